```python
import math
import jax, jax.numpy as jnp
from jax import lax
import numpy as np

D_MODEL = 1024
BATCH = 4
SEQ = 4096
DEPTH = 1

D_MIX = D_MODEL
D_CONV = D_MIX // 2
CONV_WIDTH = 3
N_ATT_HEADS = 4
ATT_VDIM = (D_MIX - D_CONV) // N_ATT_HEADS
ATT_QKDIM = ATT_VDIM // 2
D_QK = N_ATT_HEADS * 2 * ATT_QKDIM
D_V = N_ATT_HEADS * ATT_VDIM
D_IN_PROJ = 3 * D_CONV + 2 * D_QK + D_V
Q_BLOCK = 128
PEER_HEADS = 8
PEER_NKEYS = 128
PEER_NEXPERTS = PEER_NKEYS * PEER_NKEYS
PEER_TOPK = 16
PEER_DQ = 256
PEER_DHALF = PEER_DQ // 2
PEER_CHUNK = 128
LN_EPS = 1e-5
DEEPNORM_ALPHA = (2 * DEPTH) ** 0.25
DEEPNORM_BETA = (8 * DEPTH) ** -0.25

kernel_name = "hymba_conv_diffattn_peer_deepnorm"


def layer_norm(x, g, b):
    xf = x.astype(jnp.float32)
    mu = jnp.mean(xf, axis=-1, keepdims=True)
    var = jnp.mean(jnp.square(xf - mu), axis=-1, keepdims=True)
    y = (xf - mu) * lax.rsqrt(var + LN_EPS) * g.astype(jnp.float32) + b.astype(jnp.float32)
    return y.astype(x.dtype)


def rms_norm(x, g):
    xf = x.astype(jnp.float32)
    y = xf * lax.rsqrt(jnp.mean(jnp.square(xf), axis=-1, keepdims=True) + LN_EPS) * g.astype(jnp.float32)
    return y.astype(x.dtype)


def short_conv(h, w, b):
    s = h.shape[1]
    hp = jnp.pad(h, ((0, 0), (CONV_WIDTH - 1, 0), (0, 0)))
    y = b + w[CONV_WIDTH - 1] * h
    for j in range(CONV_WIDTH - 1):
        y = y + w[j] * hp[:, j:j + s]
    return y


def diff_attention(q, k, v, lam):
    b, s, h, _, d = q.shape
    nblk = s // Q_BLOCK
    scale = 1.0 / math.sqrt(d)
    qb = q.reshape(b, nblk, Q_BLOCK, h, 2, d).transpose(1, 0, 2, 3, 4, 5)
    kpos = jnp.arange(s)

    def one_block(args):
        i, qi = args
        sc = jnp.einsum('bqhpd,bkhpd->bhpqk', qi, k).astype(jnp.float32) * scale
        qpos = i * Q_BLOCK + jnp.arange(Q_BLOCK)
        mask = kpos[None, :] <= qpos[:, None]
        sc = jnp.where(mask, sc, -jnp.inf)
        p = jax.nn.softmax(sc, axis=-1)
        a = p[:, :, 0] - lam * p[:, :, 1]
        return jnp.einsum('bhqk,bkhd->bqhd', a.astype(v.dtype), v)

    out = lax.map(one_block, (jnp.arange(nblk), qb))
    return out.transpose(1, 0, 2, 3, 4).reshape(b, s, h, v.shape[-1])


def peer(x, wq, keys, u, v):
    b, s, d = x.shape
    t = b * s
    xt = x.reshape(t // PEER_CHUNK, PEER_CHUNK, d)

    def one_chunk(xc):
        q = (xc @ wq).reshape(PEER_CHUNK, PEER_HEADS, 2, PEER_DHALF)
        sc = jnp.einsum('chpd,hpnd->chpn', q, keys).astype(jnp.float32)
        s1, i1 = lax.top_k(sc[:, :, 0], PEER_TOPK)
        s2, i2 = lax.top_k(sc[:, :, 1], PEER_TOPK)
        cand = (s1[..., :, None] + s2[..., None, :]).reshape(PEER_CHUNK, PEER_HEADS, PEER_TOPK * PEER_TOPK)
        cidx = (i1[..., :, None] * PEER_NKEYS + i2[..., None, :]).reshape(PEER_CHUNK, PEER_HEADS, PEER_TOPK * PEER_TOPK)
        top_s, pos = lax.top_k(cand, PEER_TOPK)
        eidx = jnp.take_along_axis(cidx, pos, axis=-1)
        g = jax.nn.softmax(top_s, axis=-1).astype(xc.dtype)
        ue = u[eidx]
        act = jax.nn.gelu(jnp.einsum('chkd,cd->chk', ue, xc), approximate=False)
        ve = v[eidx]
        return jnp.einsum('chk,chkd->cd', g * act, ve)

    return lax.map(one_chunk, xt).reshape(b, s, d)


def setup_inputs(seed: int = 0) -> dict:
    key = jax.random.key(seed)
    ks = jax.random.split(key, 20)
    f32 = jnp.float32
    L, D = DEPTH, D_MODEL
    nrm = lambda k, shp: jax.random.normal(k, shp, f32)
    return {
        "x": nrm(ks[0], (BATCH, SEQ, D)),
        "w_in": nrm(ks[1], (L, D, D_IN_PROJ)) * D ** -0.5,
        "conv_w": nrm(ks[2], (L, CONV_WIDTH, D_CONV)) * CONV_WIDTH ** -0.5,
        "conv_b": nrm(ks[3], (L, D_CONV)) * 0.01,
        "lam_q1": nrm(ks[4], (L, ATT_QKDIM)) * 0.1,
        "lam_k1": nrm(ks[5], (L, ATT_QKDIM)) * 0.1,
        "lam_q2": nrm(ks[6], (L, ATT_QKDIM)) * 0.1,
        "lam_k2": nrm(ks[7], (L, ATT_QKDIM)) * 0.1,
        "subln_g": 1.0 + 0.01 * nrm(ks[8], (L, ATT_VDIM)),
        "w_out": nrm(ks[9], (L, D_MIX, D)) * D_MIX ** -0.5 * DEEPNORM_BETA,
        "ln1_g": 1.0 + 0.01 * nrm(ks[10], (L, D)),
        "ln1_b": 0.01 * nrm(ks[11], (L, D)),
        "peer_wq": nrm(ks[12], (L, D, PEER_HEADS * PEER_DQ)) * D ** -0.5,
        "peer_keys": nrm(ks[13], (L, PEER_HEADS, 2, PEER_NKEYS, PEER_DHALF)) * PEER_DHALF ** -0.5,
        "peer_u": nrm(ks[14], (L, PEER_NEXPERTS, D)) * D ** -0.5,
        "peer_v": nrm(ks[15], (L, PEER_NEXPERTS, D)) * DEEPNORM_BETA * PEER_HEADS ** -0.5,
        "ln2_g": 1.0 + 0.01 * nrm(ks[16], (L, D)),
        "ln2_b": 0.01 * nrm(ks[17], (L, D)),
    }


def reference(x, w_in, conv_w, conv_b, lam_q1, lam_k1, lam_q2, lam_k2, subln_g, w_out,
              ln1_g, ln1_b, peer_wq, peer_keys, peer_u, peer_v, ln2_g, ln2_b):
    b, s, _ = x.shape
    splits = [D_CONV, 2 * D_CONV, 3 * D_CONV, 3 * D_CONV + D_QK, 3 * D_CONV + 2 * D_QK]
    for l in range(DEPTH):
        proj = x @ w_in[l]
        gb, gc, hc, q, k, v = jnp.split(proj, splits, axis=-1)
        y_conv = gb * short_conv(gc * hc, conv_w[l], conv_b[l])
        lam_init = 0.8 - 0.6 * math.exp(-0.3 * l)
        lam = (jnp.exp(jnp.sum(lam_q1[l].astype(jnp.float32) * lam_k1[l].astype(jnp.float32)))
               - jnp.exp(jnp.sum(lam_q2[l].astype(jnp.float32) * lam_k2[l].astype(jnp.float32)))
               + lam_init)
        q = q.reshape(b, s, N_ATT_HEADS, 2, ATT_QKDIM)
        k = k.reshape(b, s, N_ATT_HEADS, 2, ATT_QKDIM)
        v = v.reshape(b, s, N_ATT_HEADS, ATT_VDIM)
        y_att = diff_attention(q, k, v, lam)
        y_att = rms_norm(y_att, subln_g[l]) * (1.0 - lam_init)
        y_mix = jnp.concatenate([y_conv, y_att.reshape(b, s, D_V)], axis=-1) @ w_out[l]
        x = layer_norm(DEEPNORM_ALPHA * x + y_mix, ln1_g[l], ln1_b[l])
        y_ffn = peer(x, peer_wq[l], peer_keys[l], peer_u[l], peer_v[l])
        x = layer_norm(DEEPNORM_ALPHA * x + y_ffn, ln2_g[l], ln2_b[l])
    return x
```

```python
import functools
import math

import jax
import jax.numpy as jnp
from jax import lax
from jax.experimental import pallas as pl
from jax.experimental.pallas import tpu as pltpu

F32 = jnp.float32
BF16 = jnp.bfloat16

CONV_WIDTH = 3
N_ATT_HEADS = 4
PEER_HEADS = 8
PEER_NKEYS = 128
PEER_TOPK = 16
LN_EPS = 1e-5
LOG2E = 1.4426950408889634
NEG_BIG = -1e30

SUBLANES = 8
LANES = 128
VMEM_LIMIT_BYTES = 56 * 1024 * 1024


def _nt_dot(a, b):
    return lax.dot_general(a, b, (((1,), (1,)), ((), ())), preferred_element_type=F32)


def _inproj_kernel(x_ref, w_ref, cw_ref, cb_ref, yconv_ref, q_ref, k_ref, v_ref, carry_ref,
                   *, tiles_per_seq, d_conv, d_qk, qscale):
    i = pl.program_id(0)

    @pl.when(i % tiles_per_seq == 0)
    def _():
        carry_ref[...] = jnp.zeros_like(carry_ref)

    xb = x_ref[...].astype(BF16)

    def proj(lo, width):
        return jnp.dot(xb, w_ref[:, lo:lo + width], preferred_element_type=F32)

    gb = proj(0, d_conv)
    u = proj(d_conv, d_conv) * proj(2 * d_conv, d_conv)
    tm = u.shape[0]
    row = lax.broadcasted_iota(jnp.int32, u.shape, 0)
    c1 = carry_ref[SUBLANES - 1:SUBLANES, :]
    c2 = carry_ref[SUBLANES - 2:SUBLANES - 1, :]
    u1 = jnp.where(row == 0, c1, pltpu.roll(u, 1, 0))
    u2 = jnp.where(row == 0, c2, jnp.where(row == 1, c1, pltpu.roll(u, 2, 0)))
    conv = cb_ref[...] + cw_ref[2:3, :] * u + cw_ref[1:2, :] * u1 + cw_ref[0:1, :] * u2
    carry_ref[...] = u[tm - SUBLANES:, :]
    yconv_ref[...] = (gb * conv).astype(yconv_ref.dtype)

    base = 3 * d_conv
    q_ref[...] = (proj(base, d_qk) * qscale).astype(q_ref.dtype)
    k_ref[...] = proj(base + d_qk, d_qk).astype(k_ref.dtype)
    v_ref[...] = proj(base + 2 * d_qk, v_ref.shape[1]).astype(v_ref.dtype)


def _inproj_conv(x2, w_in_b, conv_w, conv_b, *, seq, d_conv, d_qk, d_v, qscale, tm=512):
    t, d = x2.shape
    assert seq % tm == 0
    kern = functools.partial(_inproj_kernel, tiles_per_seq=seq // tm, d_conv=d_conv, d_qk=d_qk,
                             qscale=qscale)
    return pl.pallas_call(
        kern,
        out_shape=(jax.ShapeDtypeStruct((t, d_conv), BF16), jax.ShapeDtypeStruct((t, d_qk), BF16),
                   jax.ShapeDtypeStruct((t, d_qk), BF16), jax.ShapeDtypeStruct((t, d_v), BF16)),
        grid=(t // tm,),
        in_specs=[pl.BlockSpec((tm, d), lambda i: (i, 0)),
                  pl.BlockSpec(w_in_b.shape, lambda i: (0, 0)),
                  pl.BlockSpec(conv_w.shape, lambda i: (0, 0)),
                  pl.BlockSpec(conv_b.shape, lambda i: (0, 0))],
        out_specs=(pl.BlockSpec((tm, d_conv), lambda i: (i, 0)), pl.BlockSpec((tm, d_qk), lambda i: (i, 0)),
                   pl.BlockSpec((tm, d_qk), lambda i: (i, 0)), pl.BlockSpec((tm, d_v), lambda i: (i, 0))),
        scratch_shapes=[pltpu.VMEM((SUBLANES, d_conv), F32)],
        compiler_params=pltpu.CompilerParams(dimension_semantics=("arbitrary",),
                                             vmem_limit_bytes=VMEM_LIMIT_BYTES),
        name="inproj_conv",
    )(x2, w_in_b, conv_w, conv_b)


def _attn_kernel(lq1_ref, lk1_ref, lq2_ref, lk2_ref, g_ref, q_ref, k_ref, v_ref, o_ref,
                 *, tq, lam_init):
    qi = pl.program_id(2)
    q = q_ref[...]
    half = q.shape[1] // 2
    lane = lax.broadcasted_iota(jnp.int32, q.shape, 1)
    zero = jnp.zeros_like(q)
    qmaps = (jnp.where(lane < half, q, zero), jnp.where(lane >= half, q, zero))
    dv = v_ref.shape[1]

    def block(j, carry, diagonal):
        off = pl.multiple_of(j * tq, tq)
        kc = k_ref[pl.ds(off, tq), :]
        vc = v_ref[pl.ds(off, tq), :]
        out = []
        for p in range(2):
            m, l, acc = carry[p]
            s = _nt_dot(qmaps[p], kc)
            if diagonal:
                r = lax.broadcasted_iota(jnp.int32, s.shape, 0)
                c = lax.broadcasted_iota(jnp.int32, s.shape, 1)
                s = jnp.where(c <= r, s, NEG_BIG)
            m_new = jnp.maximum(m, jnp.max(s, axis=1, keepdims=True))
            alpha = jnp.exp2(m - m_new)
            pr = jnp.exp2(s - m_new)
            l = alpha * l + jnp.sum(pr, axis=1, keepdims=True)
            acc = alpha * acc + jnp.dot(pr.astype(vc.dtype), vc, preferred_element_type=F32)
            out.append((m_new, l, acc))
        return tuple(out)

    init = tuple((jnp.full((tq, 1), NEG_BIG, F32), jnp.zeros((tq, 1), F32), jnp.zeros((tq, dv), F32))
                 for _ in range(2))
    carry = lax.fori_loop(0, qi, lambda j, c: block(j, c, False), init)
    (_, l1, a1), (_, l2, a2) = block(qi, carry, True)

    lam = (jnp.exp(jnp.sum(lq1_ref[...] * lk1_ref[...], axis=1, keepdims=True))
           - jnp.exp(jnp.sum(lq2_ref[...] * lk2_ref[...], axis=1, keepdims=True)) + lam_init)
    y = a1 / l1 - lam * (a2 / l2)
    y = y * lax.rsqrt(jnp.mean(y * y, axis=1, keepdims=True) + LN_EPS) * g_ref[...]
    o_ref[...] = (y * (1.0 - lam_init)).astype(o_ref.dtype)


def _diff_attn(q, k, v, lam_q1, lam_k1, lam_q2, lam_k2, subln_g, *, batch, seq, lam_init, tq=256):
    t, d_qk = q.shape
    d_v = v.shape[1]
    n_heads = N_ATT_HEADS
    hq, hv = d_qk // n_heads, d_v // n_heads
    nq = seq // tq
    kern = functools.partial(_attn_kernel, tq=tq, lam_init=lam_init)
    small = lambda a: pl.BlockSpec(a.shape, lambda b, h, i: (0, 0))
    return pl.pallas_call(
        kern,
        out_shape=jax.ShapeDtypeStruct((t, d_v), BF16),
        grid=(batch, n_heads, nq),
        in_specs=[small(lam_q1), small(lam_k1), small(lam_q2), small(lam_k2), small(subln_g),
                  pl.BlockSpec((tq, hq), lambda b, h, i: (b * nq + i, h)),
                  pl.BlockSpec((seq, hq), lambda b, h, i: (b, h)),
                  pl.BlockSpec((seq, hv), lambda b, h, i: (b, h))],
        out_specs=pl.BlockSpec((tq, hv), lambda b, h, i: (b * nq + i, h)),
        compiler_params=pltpu.CompilerParams(dimension_semantics=("arbitrary", "arbitrary", "arbitrary"),
                                             vmem_limit_bytes=VMEM_LIMIT_BYTES),
        name="diff_attn",
    )(lam_q1, lam_k1, lam_q2, lam_k2, subln_g, q, k, v)


def _layer_norm(z, g, b):
    mu = jnp.mean(z, axis=1, keepdims=True)
    zc = z - mu
    var = jnp.mean(zc * zc, axis=1, keepdims=True)
    return zc * lax.rsqrt(var + LN_EPS) * g + b


def _outproj_kernel(x_ref, yc_ref, ya_ref, wo_ref, g_ref, b_ref, wq_ref, keys_ref,
                    x1_ref, x1b_ref, sc_ref, *, alpha, d_conv):
    ymix = (jnp.dot(yc_ref[...], wo_ref[:d_conv, :], preferred_element_type=F32)
            + jnp.dot(ya_ref[...], wo_ref[d_conv:, :], preferred_element_type=F32))
    x1 = _layer_norm(alpha * x_ref[...] + ymix, g_ref[...], b_ref[...])
    x1_ref[...] = x1
    x1b = x1.astype(BF16)
    x1b_ref[...] = x1b
    n_sub, _, dh = keys_ref.shape
    for s in range(n_sub):
        qs = jnp.dot(x1b, wq_ref[:, s * dh:(s + 1) * dh], preferred_element_type=F32).astype(BF16)
        sc_ref[s] = _nt_dot(keys_ref[s], qs)


def _outproj_query(x2, yconv, yatt, w_out_b, ln_g, ln_b, wq_b, keys_b, *, alpha, tm=512):
    t, d = x2.shape
    d_conv = yconv.shape[1]
    n_sub, nkeys, _ = keys_b.shape
    kern = functools.partial(_outproj_kernel, alpha=alpha, d_conv=d_conv)
    full = lambda a: pl.BlockSpec(a.shape, lambda i: (0,) * a.ndim)
    return pl.pallas_call(
        kern,
        out_shape=(jax.ShapeDtypeStruct((t, d), F32), jax.ShapeDtypeStruct((t, d), BF16),
                   jax.ShapeDtypeStruct((n_sub, nkeys, t), F32)),
        grid=(t // tm,),
        in_specs=[pl.BlockSpec((tm, d), lambda i: (i, 0)),
                  pl.BlockSpec((tm, d_conv), lambda i: (i, 0)),
                  pl.BlockSpec((tm, yatt.shape[1]), lambda i: (i, 0)),
                  full(w_out_b), full(ln_g), full(ln_b), full(wq_b), full(keys_b)],
        out_specs=(pl.BlockSpec((tm, d), lambda i: (i, 0)), pl.BlockSpec((tm, d), lambda i: (i, 0)),
                   pl.BlockSpec((n_sub, nkeys, tm), lambda i: (0, 0, i))),
        compiler_params=pltpu.CompilerParams(dimension_semantics=("arbitrary",),
                                             vmem_limit_bytes=VMEM_LIMIT_BYTES),
        name="outproj_query",
    )(x2, yconv, yatt, w_out_b, ln_g, ln_b, wq_b, keys_b)


def _oddeven_merge_sort_pairs(n):
    pairs = []
    p = 1
    while p < n:
        k = p
        while k >= 1:
            for j in range(k % p, n - k, 2 * k):
                for i in range(min(k, n - j - k)):
                    if (i + j) // (2 * p) == (i + j + k) // (2 * p):
                        pairs.append((i + j, i + j + k))
            k //= 2
        p *= 2
    return pairs


def _sort_desc(vals):
    vals = list(vals)
    for i, j in _oddeven_merge_sort_pairs(len(vals)):
        hi = jnp.maximum(vals[i], vals[j])
        lo = jnp.minimum(vals[i], vals[j])
        vals[i], vals[j] = hi, lo
    return vals


def _bitonic_merge_desc(vals):
    vals = list(vals)
    n = len(vals)
    d = n // 2
    while d >= 1:
        for i in range(n):
            if i & d == 0:
                hi = jnp.maximum(vals[i], vals[i + d])
                lo = jnp.minimum(vals[i], vals[i + d])
                vals[i], vals[i + d] = hi, lo
        d //= 2
    return vals


def _top_sorted(s, k):
    n = s.shape[0]
    assert n == k * SUBLANES
    slabs = _sort_desc([s[a * SUBLANES:(a + 1) * SUBLANES, :] for a in range(k)])
    shift = SUBLANES // 2
    while shift >= 1:
        other = [pltpu.roll(x, shift, 0) for x in slabs]
        slabs = _bitonic_merge_desc([jnp.maximum(slabs[r], other[k - 1 - r]) for r in range(k)])
        shift //= 2
    return [x[0:1, :] for x in slabs]


def _next_below(s, kth, k):
    ge = s >= kth
    cnt = jnp.sum(jnp.where(ge, 1.0, 0.0), axis=0, keepdims=True)
    below = jnp.max(jnp.where(ge, NEG_BIG, s), axis=0, keepdims=True)
    return jnp.where(cnt > k, kth, below)


def _gate_kernel(sc_ref, e1_ref, e2_ref, th_ref, *, topk):
    n_sub = sc_ref.shape[0]
    heads = n_sub // 2
    k = topk
    tops = []
    for s_idx in range(n_sub):
        s = sc_ref[s_idx]
        rows = _top_sorted(s, k)
        rows.append(_next_below(s, rows[k - 1], k))
        tops.append([jnp.exp(r - rows[0]) for r in rows])
    a = [jnp.concatenate([tops[2 * h][r] for h in range(heads)], axis=0) for r in range(k + 1)]
    b = [jnp.concatenate([tops[2 * h + 1][r] for h in range(heads)], axis=0) for r in range(k + 1)]
    cands = [a[r] * b[c] for r in range(k + 1) for c in range(k + 1) if (r + 1) * (c + 1) <= k + 1]
    n_pad = 1
    while n_pad < len(cands):
        n_pad *= 2
    pad = jnp.full_like(cands[0], -1.0)
    srt = _sort_desc(cands + [pad] * (n_pad - len(cands)))
    tstar = 0.5 * (srt[k - 1] + srt[k])
    z = jnp.zeros_like(tstar)
    for c in cands:
        z = z + jnp.where(c >= tstar, c, 0.0)
    zinv = 1.0 / z
    tz = tstar * zinv
    for h in range(heads):
        e1 = jnp.exp(sc_ref[2 * h] - jnp.max(sc_ref[2 * h], axis=0, keepdims=True))
        e2 = jnp.exp(sc_ref[2 * h + 1] - jnp.max(sc_ref[2 * h + 1], axis=0, keepdims=True))
        e1_ref[h] = e1
        e2_ref[h] = e2 * zinv[h:h + 1, :]
        th_ref[h] = tz[h:h + 1, :] / e1


def _peer_gate(sc, *, topk, tl=256):
    n_sub, nkeys, t = sc.shape
    heads = n_sub // 2
    out = jax.ShapeDtypeStruct((heads, nkeys, t), F32)
    spec = pl.BlockSpec((heads, nkeys, tl), lambda i: (0, 0, i))
    return pl.pallas_call(
        functools.partial(_gate_kernel, topk=topk),
        out_shape=(out, out, out),
        grid=(t // tl,),
        in_specs=[pl.BlockSpec((n_sub, nkeys, tl), lambda i: (0, 0, i))],
        out_specs=(spec, spec, spec),
        compiler_params=pltpu.CompilerParams(dimension_semantics=("arbitrary",),
                                             vmem_limit_bytes=VMEM_LIMIT_BYTES),
        name="peer_gate",
    )(sc)


def _peer_kernel(x1b_ref, x1_ref, u_ref, vt_ref, e1_ref, e2_ref, th_ref, g_ref, b_ref, o_ref,
                 acc_ref, act_ref, ga_ref, *, alpha, nkeys):
    e = pl.program_id(1)
    heads = e2_ref.shape[0]
    n_i = e1_ref.shape[1]
    tm = x1b_ref.shape[0]
    n_lt = tm // LANES

    @pl.when(e == 0)
    def _():
        acc_ref[...] = jnp.zeros_like(acc_ref)

    act_ref[...] = _nt_dot(u_ref[...], x1b_ref[...])

    def chunk(il, c, _):
        lanes = pl.ds(pl.multiple_of(c * LANES, LANES), LANES)
        rows = pl.ds(il * nkeys, nkeys)
        gate = jnp.zeros((nkeys, LANES), F32)
        for h in range(heads):
            e2 = e2_ref[h, :, lanes]
            th = th_ref[h, il:il + 1, lanes]
            e1 = e1_ref[h, il:il + 1, lanes]
            gate = gate + jnp.where(e2 >= th, e2, 0.0) * e1
        a = act_ref[rows, lanes]
        gelu = 0.5 * a * (1.0 + lax.erf(a * math.sqrt(0.5)))
        ga_ref[rows, lanes] = (gate * gelu).astype(ga_ref.dtype)
        return 0

    for il in range(n_i):
        lax.fori_loop(0, n_lt, functools.partial(chunk, il), 0)
    acc_ref[...] += jnp.dot(vt_ref[...], ga_ref[...], preferred_element_type=F32)

    @pl.when(e == pl.num_programs(1) - 1)
    def _():
        z = alpha * x1_ref[...] + acc_ref[...].T
        o_ref[...] = _layer_norm(z, g_ref[...], b_ref[...])


def _peer_dense(x1b, x1, u_b, vt_b, e1, e2, th, ln_g, ln_b, *, alpha, tm=512, be=1024):
    t, d = x1.shape
    n_exp = u_b.shape[0]
    heads, nkeys, _ = e2.shape
    n_i = be // nkeys
    kern = functools.partial(_peer_kernel, alpha=alpha, nkeys=nkeys)
    return pl.pallas_call(
        kern,
        out_shape=jax.ShapeDtypeStruct((t, d), F32),
        grid=(t // tm, n_exp // be),
        in_specs=[pl.BlockSpec((tm, d), lambda i, e: (i, 0)),
                  pl.BlockSpec((tm, d), lambda i, e: (i, 0)),
                  pl.BlockSpec((be, d), lambda i, e: (e, 0)),
                  pl.BlockSpec((d, be), lambda i, e: (0, e)),
                  pl.BlockSpec((heads, n_i, tm), lambda i, e: (0, e, i)),
                  pl.BlockSpec((heads, nkeys, tm), lambda i, e: (0, 0, i)),
                  pl.BlockSpec((heads, n_i, tm), lambda i, e: (0, e, i)),
                  pl.BlockSpec(ln_g.shape, lambda i, e: (0, 0)),
                  pl.BlockSpec(ln_b.shape, lambda i, e: (0, 0))],
        out_specs=pl.BlockSpec((tm, d), lambda i, e: (i, 0)),
        scratch_shapes=[pltpu.VMEM((d, tm), F32), pltpu.VMEM((be, tm), F32), pltpu.VMEM((be, tm), BF16)],
        compiler_params=pltpu.CompilerParams(dimension_semantics=("arbitrary", "arbitrary"),
                                             vmem_limit_bytes=VMEM_LIMIT_BYTES),
        name="peer_dense",
    )(x1b, x1, u_b, vt_b, e1, e2, th, ln_g, ln_b)


def kernel(x, w_in, conv_w, conv_b, lam_q1, lam_k1, lam_q2, lam_k2, subln_g, w_out, ln1_g, ln1_b,
           peer_wq, peer_keys, peer_u, peer_v, ln2_g, ln2_b):
    batch, seq, d = x.shape
    depth = w_in.shape[0]
    d_conv = conv_w.shape[2]
    att_vdim = subln_g.shape[1]
    d_v = N_ATT_HEADS * att_vdim
    d_qk = (w_in.shape[2] - 3 * d_conv - d_v) // 2
    qk_dim = d_qk // (2 * N_ATT_HEADS)
    alpha = (2 * depth) ** 0.25
    heads, _, nkeys, dhalf = peer_keys.shape[1:]
    row = lambda a: a.reshape(1, -1)

    x2 = x.reshape(batch * seq, d)
    for l in range(depth):
        lam_init = 0.8 - 0.6 * math.exp(-0.3 * l)
        qscale = LOG2E / math.sqrt(qk_dim)
        yconv, q, k, v = _inproj_conv(x2, w_in[l].astype(BF16), conv_w[l], row(conv_b[l]), seq=seq,
                                      d_conv=d_conv, d_qk=d_qk, d_v=d_v, qscale=qscale)
        yatt = _diff_attn(q, k, v, row(lam_q1[l]), row(lam_k1[l]), row(lam_q2[l]), row(lam_k2[l]),
                          row(subln_g[l]), batch=batch, seq=seq, lam_init=lam_init)
        x1, x1b, sc = _outproj_query(x2, yconv, yatt, w_out[l].astype(BF16), row(ln1_g[l]), row(ln1_b[l]),
                                     peer_wq[l].astype(BF16),
                                     peer_keys[l].reshape(heads * 2, nkeys, dhalf).astype(BF16), alpha=alpha)
        e1, e2, th = _peer_gate(sc, topk=PEER_TOPK)
        x2 = _peer_dense(x1b, x1, peer_u[l].astype(BF16), peer_v[l].T.astype(BF16), e1, e2, th,
                         row(ln2_g[l]), row(ln2_b[l]), alpha=alpha)
    return x2.reshape(batch, seq, d)
```

```python
import functools
import math

import jax
import jax.numpy as jnp
from jax import lax
from jax.experimental import pallas as pl
from jax.experimental.pallas import tpu as pltpu

F32 = jnp.float32
BF16 = jnp.bfloat16

CONV_WIDTH = 3
N_ATT_HEADS = 4
PEER_HEADS = 8
PEER_NKEYS = 128
PEER_TOPK = 16
LN_EPS = 1e-5
LOG2E = 1.4426950408889634
NEG_BIG = -1e30

SUBLANES = 8
LANES = 128
VMEM_LIMIT_BYTES = 56 * 1024 * 1024


def _nt_dot(a, b):
    return lax.dot_general(a, b, (((1,), (1,)), ((), ())), preferred_element_type=F32)


def _inproj_kernel(x_ref, w_ref, cw_ref, cb_ref, yconv_ref, q_ref, k_ref, v_ref, carry_ref,
                   *, tiles_per_seq, d_conv, d_qk, n_heads, qscale):
    i = pl.program_id(0)

    @pl.when(i % tiles_per_seq == 0)
    def _():
        carry_ref[...] = jnp.zeros_like(carry_ref)

    xb = x_ref[...].astype(BF16)

    def proj(lo, width):
        return jnp.dot(xb, w_ref[:, lo:lo + width], preferred_element_type=F32)

    gb = proj(0, d_conv)
    u = proj(d_conv, d_conv) * proj(2 * d_conv, d_conv)
    tm = u.shape[0]
    row = lax.broadcasted_iota(jnp.int32, u.shape, 0)
    c1 = carry_ref[SUBLANES - 1:SUBLANES, :]
    c2 = carry_ref[SUBLANES - 2:SUBLANES - 1, :]
    u1 = jnp.where(row == 0, c1, pltpu.roll(u, 1, 0))
    u2 = jnp.where(row == 0, c2, jnp.where(row == 1, c1, pltpu.roll(u, 2, 0)))
    conv = cb_ref[...] + cw_ref[2:3, :] * u + cw_ref[1:2, :] * u1 + cw_ref[0:1, :] * u2
    carry_ref[...] = u[tm - SUBLANES:, :]
    yconv_ref[...] = (gb * conv).astype(yconv_ref.dtype)

    base = 3 * d_conv
    q = (proj(base, d_qk) * qscale).astype(q_ref.dtype)
    hq = d_qk // n_heads
    lane = lax.broadcasted_iota(jnp.int32, (tm, hq), 1)
    zero = jnp.zeros((tm, hq), q_ref.dtype)
    for h in range(n_heads):
        qh = q[:, h * hq:(h + 1) * hq]
        q_ref[:, (2 * h) * hq:(2 * h + 1) * hq] = jnp.where(lane < hq // 2, qh, zero)
        q_ref[:, (2 * h + 1) * hq:(2 * h + 2) * hq] = jnp.where(lane >= hq // 2, qh, zero)
    k_ref[...] = proj(base + d_qk, d_qk).astype(k_ref.dtype)
    v_ref[...] = proj(base + 2 * d_qk, v_ref.shape[1]).astype(v_ref.dtype)


def _inproj_conv(x2, w_in_b, conv_w, conv_b, *, seq, d_conv, d_qk, d_v, qscale, tm=512):
    t, d = x2.shape
    assert seq % tm == 0
    kern = functools.partial(_inproj_kernel, tiles_per_seq=seq // tm, d_conv=d_conv, d_qk=d_qk,
                             n_heads=N_ATT_HEADS, qscale=qscale)
    return pl.pallas_call(
        kern,
        out_shape=(jax.ShapeDtypeStruct((t, d_conv), BF16), jax.ShapeDtypeStruct((t, 2 * d_qk), BF16),
                   jax.ShapeDtypeStruct((t, d_qk), BF16), jax.ShapeDtypeStruct((t, d_v), BF16)),
        grid=(t // tm,),
        in_specs=[pl.BlockSpec((tm, d), lambda i: (i, 0)),
                  pl.BlockSpec(w_in_b.shape, lambda i: (0, 0)),
                  pl.BlockSpec(conv_w.shape, lambda i: (0, 0)),
                  pl.BlockSpec(conv_b.shape, lambda i: (0, 0))],
        out_specs=(pl.BlockSpec((tm, d_conv), lambda i: (i, 0)), pl.BlockSpec((tm, 2 * d_qk), lambda i: (i, 0)),
                   pl.BlockSpec((tm, d_qk), lambda i: (i, 0)), pl.BlockSpec((tm, d_v), lambda i: (i, 0))),
        scratch_shapes=[pltpu.VMEM((SUBLANES, d_conv), F32)],
        compiler_params=pltpu.CompilerParams(dimension_semantics=("arbitrary",),
                                             vmem_limit_bytes=VMEM_LIMIT_BYTES),
        name="inproj_conv",
    )(x2, w_in_b, conv_w, conv_b)


def _attn_kernel(lq1_ref, lk1_ref, lq2_ref, lk2_ref, g_ref, q_ref, k_ref, v_ref, o_ref,
                 m_ref, l_ref, acc_ref, *, tq, n_heads, lam_init):
    qi = pl.program_id(1)
    hw = k_ref.shape[1] // n_heads
    n_chains = 2 * n_heads

    m_ref[...] = jnp.full(m_ref.shape, NEG_BIG, F32)
    l_ref[...] = jnp.zeros(l_ref.shape, F32)
    acc_ref[...] = jnp.zeros(acc_ref.shape, F32)

    def block(j, diagonal):
        off = pl.multiple_of(j * tq, tq)
        n_slab = tq // LANES
        scores = [_nt_dot(q_ref[:, c * hw:(c + 1) * hw], k_ref[pl.ds(off, tq), (c // 2) * hw:(c // 2 + 1) * hw])
                  for c in range(n_chains)]
        probs, alphas = [], []
        for c in range(n_chains):
            slabs = [scores[c][:, t * LANES:(t + 1) * LANES] for t in range(n_slab)]
            if diagonal:
                r = lax.broadcasted_iota(jnp.int32, (tq, LANES), 0)
                col = lax.broadcasted_iota(jnp.int32, (tq, LANES), 1)
                slabs = [jnp.where(col + t * LANES <= r, sl, NEG_BIG) for t, sl in enumerate(slabs)]
            mx = slabs[0]
            for sl in slabs[1:]:
                mx = jnp.maximum(mx, sl)
            m_old = m_ref[c]
            m_new = jnp.maximum(m_old, jnp.broadcast_to(jnp.max(mx, axis=1, keepdims=True), m_old.shape))
            alpha = jnp.exp2(m_old - m_new)
            ps = [jnp.exp2(sl - m_new) for sl in slabs]
            part = ps[0]
            for p in ps[1:]:
                part = part + p
            m_ref[c] = m_new
            l_ref[c] = alpha * l_ref[c] + part
            probs.append(jnp.concatenate(ps, axis=1).astype(v_ref.dtype))
            alphas.append(alpha)
        for c in range(n_chains):
            vc = v_ref[pl.ds(off, tq), (c // 2) * hw:(c // 2 + 1) * hw]
            acc_ref[c] = alphas[c] * acc_ref[c] + jnp.dot(probs[c], vc, preferred_element_type=F32)

    def body(j, carry):
        block(j, False)
        return carry

    lax.fori_loop(0, qi, body, 0)
    block(qi, True)

    lam = (jnp.exp(jnp.sum(lq1_ref[...] * lk1_ref[...], axis=1, keepdims=True))
           - jnp.exp(jnp.sum(lq2_ref[...] * lk2_ref[...], axis=1, keepdims=True)) + lam_init)
    for h in range(n_heads):
        o1 = acc_ref[2 * h] / jnp.sum(l_ref[2 * h], axis=1, keepdims=True)
        o2 = acc_ref[2 * h + 1] / jnp.sum(l_ref[2 * h + 1], axis=1, keepdims=True)
        y = o1 - lam * o2
        y = y * lax.rsqrt(jnp.mean(y * y, axis=1, keepdims=True) + LN_EPS) * g_ref[...]
        o_ref[:, h * hw:(h + 1) * hw] = (y * (1.0 - lam_init)).astype(o_ref.dtype)


def _diff_attn(q, k, v, lam_q1, lam_k1, lam_q2, lam_k2, subln_g, *, batch, seq, lam_init, tq=256):
    t, d_k = k.shape
    d_v = v.shape[1]
    n_heads = N_ATT_HEADS
    assert d_k == d_v and q.shape[1] == 2 * d_k
    nq = seq // tq
    kern = functools.partial(_attn_kernel, tq=tq, n_heads=n_heads, lam_init=lam_init)
    small = lambda a: pl.BlockSpec(a.shape, lambda b, i: (0, 0))
    n_chains = 2 * n_heads
    return pl.pallas_call(
        kern,
        out_shape=jax.ShapeDtypeStruct((t, d_v), BF16),
        grid=(batch, nq),
        in_specs=[small(lam_q1), small(lam_k1), small(lam_q2), small(lam_k2), small(subln_g),
                  pl.BlockSpec((tq, 2 * d_k), lambda b, i: (b * nq + i, 0)),
                  pl.BlockSpec((seq, d_k), lambda b, i: (b, 0)),
                  pl.BlockSpec((seq, d_v), lambda b, i: (b, 0))],
        out_specs=pl.BlockSpec((tq, d_v), lambda b, i: (b * nq + i, 0)),
        scratch_shapes=[pltpu.VMEM((n_chains, tq, LANES), F32), pltpu.VMEM((n_chains, tq, LANES), F32),
                        pltpu.VMEM((n_chains, tq, d_v // n_heads), F32)],
        compiler_params=pltpu.CompilerParams(dimension_semantics=("arbitrary", "arbitrary"),
                                             vmem_limit_bytes=VMEM_LIMIT_BYTES),
        name="diff_attn",
    )(lam_q1, lam_k1, lam_q2, lam_k2, subln_g, q, k, v)


def _layer_norm(z, g, b):
    mu = jnp.mean(z, axis=1, keepdims=True)
    zc = z - mu
    var = jnp.mean(zc * zc, axis=1, keepdims=True)
    return zc * lax.rsqrt(var + LN_EPS) * g + b


def _outproj_kernel(x_ref, yc_ref, ya_ref, wo_ref, g_ref, b_ref, wq_ref, keys_ref,
                    x1_ref, x1b_ref, sc_ref, *, alpha, d_conv):
    ymix = (jnp.dot(yc_ref[...], wo_ref[:d_conv, :], preferred_element_type=F32)
            + jnp.dot(ya_ref[...], wo_ref[d_conv:, :], preferred_element_type=F32))
    x1 = _layer_norm(alpha * x_ref[...] + ymix, g_ref[...], b_ref[...])
    x1_ref[...] = x1
    x1b = x1.astype(BF16)
    x1b_ref[...] = x1b
    n_sub, _, dh = keys_ref.shape
    for s in range(n_sub):
        qs = jnp.dot(x1b, wq_ref[:, s * dh:(s + 1) * dh], preferred_element_type=F32).astype(BF16)
        sc_ref[s] = _nt_dot(keys_ref[s], qs)


def _outproj_query(x2, yconv, yatt, w_out_b, ln_g, ln_b, wq_b, keys_b, *, alpha, tm=512):
    t, d = x2.shape
    d_conv = yconv.shape[1]
    n_sub, nkeys, _ = keys_b.shape
    kern = functools.partial(_outproj_kernel, alpha=alpha, d_conv=d_conv)
    full = lambda a: pl.BlockSpec(a.shape, lambda i: (0,) * a.ndim)
    return pl.pallas_call(
        kern,
        out_shape=(jax.ShapeDtypeStruct((t, d), F32), jax.ShapeDtypeStruct((t, d), BF16),
                   jax.ShapeDtypeStruct((n_sub, nkeys, t), F32)),
        grid=(t // tm,),
        in_specs=[pl.BlockSpec((tm, d), lambda i: (i, 0)),
                  pl.BlockSpec((tm, d_conv), lambda i: (i, 0)),
                  pl.BlockSpec((tm, yatt.shape[1]), lambda i: (i, 0)),
                  full(w_out_b), full(ln_g), full(ln_b), full(wq_b), full(keys_b)],
        out_specs=(pl.BlockSpec((tm, d), lambda i: (i, 0)), pl.BlockSpec((tm, d), lambda i: (i, 0)),
                   pl.BlockSpec((n_sub, nkeys, tm), lambda i: (0, 0, i))),
        compiler_params=pltpu.CompilerParams(dimension_semantics=("arbitrary",),
                                             vmem_limit_bytes=VMEM_LIMIT_BYTES),
        name="outproj_query",
    )(x2, yconv, yatt, w_out_b, ln_g, ln_b, wq_b, keys_b)


def _oddeven_merge_sort_pairs(n):
    pairs = []
    p = 1
    while p < n:
        k = p
        while k >= 1:
            for j in range(k % p, n - k, 2 * k):
                for i in range(min(k, n - j - k)):
                    if (i + j) // (2 * p) == (i + j + k) // (2 * p):
                        pairs.append((i + j, i + j + k))
            k //= 2
        p *= 2
    return pairs


def _sort_desc(vals):
    vals = list(vals)
    for i, j in _oddeven_merge_sort_pairs(len(vals)):
        hi = jnp.maximum(vals[i], vals[j])
        lo = jnp.minimum(vals[i], vals[j])
        vals[i], vals[j] = hi, lo
    return vals


def _bitonic_merge_desc(vals):
    vals = list(vals)
    n = len(vals)
    d = n // 2
    while d >= 1:
        for i in range(n):
            if i & d == 0:
                hi = jnp.maximum(vals[i], vals[i + d])
                lo = jnp.minimum(vals[i], vals[i + d])
                vals[i], vals[i + d] = hi, lo
        d //= 2
    return vals


def _top_sorted(s, k):
    n = s.shape[0]
    assert n == k * SUBLANES
    slabs = _sort_desc([s[a * SUBLANES:(a + 1) * SUBLANES, :] for a in range(k)])
    shift = SUBLANES // 2
    while shift >= 1:
        other = [pltpu.roll(x, shift, 0) for x in slabs]
        slabs = _bitonic_merge_desc([jnp.maximum(slabs[r], other[k - 1 - r]) for r in range(k)])
        shift //= 2
    return [x[0:1, :] for x in slabs]


def _next_below(s, kth, k):
    ge = s >= kth
    cnt = jnp.sum(jnp.where(ge, 1.0, 0.0), axis=0, keepdims=True)
    below = jnp.max(jnp.where(ge, NEG_BIG, s), axis=0, keepdims=True)
    return jnp.where(cnt > k, kth, below)


def _gate_kernel(sc_ref, w1_ref, nsel_ref, w2_ref, rank_ref, *, topk):
    n_sub = sc_ref.shape[0]
    heads = n_sub // 2
    k = topk
    tops = []
    for s_idx in range(n_sub):
        s = sc_ref[s_idx]
        rows = _top_sorted(s, k)
        rows.append(_next_below(s, rows[k - 1], k))
        tops.append([jnp.exp(r - rows[0]) for r in rows])
    a = [jnp.concatenate([tops[2 * h][r] for h in range(heads)], axis=0) for r in range(k + 1)]
    b = [jnp.concatenate([tops[2 * h + 1][r] for h in range(heads)], axis=0) for r in range(k + 1)]
    cands = [a[r] * b[c] for r in range(k + 1) for c in range(k + 1) if (r + 1) * (c + 1) <= k + 1]
    n_pad = 1
    while n_pad < len(cands):
        n_pad *= 2
    pad = jnp.full_like(cands[0], -1.0)
    srt = _sort_desc(cands + [pad] * (n_pad - len(cands)))
    tstar = 0.5 * (srt[k - 1] + srt[k])
    z = jnp.zeros_like(tstar)
    for c in cands:
        z = z + jnp.where(c >= tstar, c, 0.0)
    zinv = 1.0 / z
    for h in range(heads):
        e1 = jnp.exp(sc_ref[2 * h] - jnp.max(sc_ref[2 * h], axis=0, keepdims=True))
        e2 = jnp.exp(sc_ref[2 * h + 1] - jnp.max(sc_ref[2 * h + 1], axis=0, keepdims=True))
        theta = tstar[h:h + 1, :] / e1
        rank = jnp.zeros_like(e2)
        nsel = jnp.zeros_like(e1)
        for r in range(k):
            top_r = tops[2 * h + 1][r]
            rank = rank + jnp.where(e2 < top_r, 1.0, 0.0)
            nsel = nsel + jnp.where(top_r >= theta, 1.0, 0.0)
        w1_ref[h] = e1
        nsel_ref[h] = nsel
        w2_ref[h] = (e2 * zinv[h:h + 1, :]).astype(w2_ref.dtype)
        rank_ref[h] = rank.astype(rank_ref.dtype)


def _peer_gate(sc, *, topk, tl=256):
    n_sub, nkeys, t = sc.shape
    heads = n_sub // 2
    out_f32 = jax.ShapeDtypeStruct((heads, nkeys, t), F32)
    out_b16 = jax.ShapeDtypeStruct((heads, nkeys, t), BF16)
    spec = pl.BlockSpec((heads, nkeys, tl), lambda i: (0, 0, i))
    return pl.pallas_call(
        functools.partial(_gate_kernel, topk=topk),
        out_shape=(out_f32, out_f32, out_b16, out_b16),
        grid=(t // tl,),
        in_specs=[pl.BlockSpec((n_sub, nkeys, tl), lambda i: (0, 0, i))],
        out_specs=(spec, spec, spec, spec),
        compiler_params=pltpu.CompilerParams(dimension_semantics=("arbitrary",),
                                             vmem_limit_bytes=VMEM_LIMIT_BYTES),
        name="peer_gate",
    )(sc)


def _peer_kernel(x1b_ref, x1_ref, u_ref, vt_ref, w1_ref, nsel_ref, w2_ref, rank_ref, g_ref, b_ref, o_ref,
                 acc_ref, act_ref, ga_ref, w2s_ref, ranks_ref, *, alpha, nkeys):
    e = pl.program_id(1)
    heads = w2_ref.shape[0]
    n_i = w1_ref.shape[1]
    tm = x1b_ref.shape[0]
    cw = LANES
    n_lt = tm // cw

    @pl.when(e == 0)
    def _():
        acc_ref[...] = jnp.zeros_like(acc_ref)
        w2s_ref[...] = w2_ref[...]
        ranks_ref[...] = rank_ref[...]

    act_ref[...] = _nt_dot(u_ref[...], x1b_ref[...])

    def chunk(il, c):
        lanes = pl.ds(c * cw, cw)
        rows = pl.ds(il * nkeys, nkeys)
        gate = jnp.zeros((nkeys, cw), BF16)
        for h in range(heads):
            rank = ranks_ref[h, :, lanes]
            w2 = w2s_ref[h, :, lanes]
            nsel = nsel_ref[h, il:il + 1, lanes].astype(BF16)
            w1 = w1_ref[h, il:il + 1, lanes].astype(BF16)
            gate = gate + jnp.where(rank < nsel, w2, jnp.zeros_like(w2)) * w1
        a = act_ref[rows, lanes]
        gelu = 0.5 * a * (1.0 + lax.erf(a * math.sqrt(0.5)))
        ga_ref[rows, lanes] = gate * gelu.astype(BF16)

    for il in range(n_i):
        for c in range(n_lt):
            chunk(il, c)
    acc_ref[...] += jnp.dot(vt_ref[...], ga_ref[...], preferred_element_type=F32)

    @pl.when(e == pl.num_programs(1) - 1)
    def _():
        z = alpha * x1_ref[...] + acc_ref[...].T
        o_ref[...] = _layer_norm(z, g_ref[...], b_ref[...])


def _peer_dense(x1b, x1, u_b, vt_b, w1, nsel, w2, rank, ln_g, ln_b, *, alpha, tm=512, be=1024):
    t, d = x1.shape
    n_exp = u_b.shape[0]
    heads, nkeys, _ = w2.shape
    n_i = be // nkeys
    kern = functools.partial(_peer_kernel, alpha=alpha, nkeys=nkeys)
    return pl.pallas_call(
        kern,
        out_shape=jax.ShapeDtypeStruct((t, d), F32),
        grid=(t // tm, n_exp // be),
        in_specs=[pl.BlockSpec((tm, d), lambda i, e: (i, 0)),
                  pl.BlockSpec((tm, d), lambda i, e: (i, 0)),
                  pl.BlockSpec((be, d), lambda i, e: (e, 0)),
                  pl.BlockSpec((d, be), lambda i, e: (0, e)),
                  pl.BlockSpec((heads, n_i, tm), lambda i, e: (0, e, i)),
                  pl.BlockSpec((heads, n_i, tm), lambda i, e: (0, e, i)),
                  pl.BlockSpec((heads, nkeys, tm), lambda i, e: (0, 0, i)),
                  pl.BlockSpec((heads, nkeys, tm), lambda i, e: (0, 0, i)),
                  pl.BlockSpec(ln_g.shape, lambda i, e: (0, 0)),
                  pl.BlockSpec(ln_b.shape, lambda i, e: (0, 0))],
        out_specs=pl.BlockSpec((tm, d), lambda i, e: (i, 0)),
        scratch_shapes=[pltpu.VMEM((d, tm), F32), pltpu.VMEM((be, tm), F32), pltpu.VMEM((be, tm), BF16),
                        pltpu.VMEM((heads, nkeys, tm), BF16), pltpu.VMEM((heads, nkeys, tm), BF16)],
        compiler_params=pltpu.CompilerParams(dimension_semantics=("arbitrary", "arbitrary"),
                                             vmem_limit_bytes=VMEM_LIMIT_BYTES),
        name="peer_dense",
    )(x1b, x1, u_b, vt_b, w1, nsel, w2, rank, ln_g, ln_b)


def kernel(x, w_in, conv_w, conv_b, lam_q1, lam_k1, lam_q2, lam_k2, subln_g, w_out, ln1_g, ln1_b,
           peer_wq, peer_keys, peer_u, peer_v, ln2_g, ln2_b):
    batch, seq, d = x.shape
    depth = w_in.shape[0]
    d_conv = conv_w.shape[2]
    att_vdim = subln_g.shape[1]
    d_v = N_ATT_HEADS * att_vdim
    d_qk = (w_in.shape[2] - 3 * d_conv - d_v) // 2
    qk_dim = d_qk // (2 * N_ATT_HEADS)
    alpha = (2 * depth) ** 0.25
    heads, _, nkeys, dhalf = peer_keys.shape[1:]
    row = lambda a: a.reshape(1, -1)

    x2 = x.reshape(batch * seq, d)
    for l in range(depth):
        lam_init = 0.8 - 0.6 * math.exp(-0.3 * l)
        qscale = LOG2E / math.sqrt(qk_dim)
        yconv, q, k, v = _inproj_conv(x2, w_in[l].astype(BF16), conv_w[l], row(conv_b[l]), seq=seq,
                                      d_conv=d_conv, d_qk=d_qk, d_v=d_v, qscale=qscale)
        yatt = _diff_attn(q, k, v, row(lam_q1[l]), row(lam_k1[l]), row(lam_q2[l]), row(lam_k2[l]),
                          row(subln_g[l]), batch=batch, seq=seq, lam_init=lam_init)
        x1, x1b, sc = _outproj_query(x2, yconv, yatt, w_out[l].astype(BF16), row(ln1_g[l]), row(ln1_b[l]),
                                     peer_wq[l].astype(BF16),
                                     peer_keys[l].reshape(heads * 2, nkeys, dhalf).astype(BF16), alpha=alpha)
        w1, nsel, w2, rank = _peer_gate(sc, topk=PEER_TOPK)
        x2 = _peer_dense(x1b, x1, peer_u[l].astype(BF16), peer_v[l].T.astype(BF16), w1, nsel, w2, rank,
                         row(ln2_g[l]), row(ln2_b[l]), alpha=alpha)
    return x2.reshape(batch, seq, d)
```

```python
import functools
import math

import jax
import jax.numpy as jnp
from jax import lax
from jax.experimental import pallas as pl
from jax.experimental.pallas import tpu as pltpu

F32 = jnp.float32
BF16 = jnp.bfloat16

CONV_WIDTH = 3
N_ATT_HEADS = 4
PEER_HEADS = 8
PEER_NKEYS = 128
PEER_TOPK = 16
PEER_EXPERT_BLOCK = 1024
LN_EPS = 1e-5
LOG2E = 1.4426950408889634
NEG_BIG = -1e30

SUBLANES = 8
LANES = 128
VMEM_LIMIT_BYTES = 56 * 1024 * 1024


def _nt_dot(a, b):
    return lax.dot_general(a, b, (((1,), (1,)), ((), ())), preferred_element_type=F32)


def _inproj_kernel(x_ref, w_ref, cw_ref, cb_ref, yconv_ref, q_ref, k_ref, v_ref, carry_ref,
                   *, tiles_per_seq, d_conv, d_qk, n_heads, qscale):
    i = pl.program_id(0)

    @pl.when(i % tiles_per_seq == 0)
    def _():
        carry_ref[...] = jnp.zeros_like(carry_ref)

    xb = x_ref[...].astype(BF16)

    def proj(lo, width):
        return jnp.dot(xb, w_ref[:, lo:lo + width], preferred_element_type=F32)

    gb = proj(0, d_conv)
    u = proj(d_conv, d_conv) * proj(2 * d_conv, d_conv)
    tm = u.shape[0]
    row = lax.broadcasted_iota(jnp.int32, u.shape, 0)
    c1 = carry_ref[SUBLANES - 1:SUBLANES, :]
    c2 = carry_ref[SUBLANES - 2:SUBLANES - 1, :]
    u1 = jnp.where(row == 0, c1, pltpu.roll(u, 1, 0))
    u2 = jnp.where(row == 0, c2, jnp.where(row == 1, c1, pltpu.roll(u, 2, 0)))
    conv = cb_ref[...] + cw_ref[2:3, :] * u + cw_ref[1:2, :] * u1 + cw_ref[0:1, :] * u2
    carry_ref[...] = u[tm - SUBLANES:, :]
    yconv_ref[...] = (gb * conv).astype(yconv_ref.dtype)

    base = 3 * d_conv
    q = (proj(base, d_qk) * qscale).astype(q_ref.dtype)
    hq = d_qk // n_heads
    lane = lax.broadcasted_iota(jnp.int32, (tm, hq), 1)
    zero = jnp.zeros((tm, hq), q_ref.dtype)
    for h in range(n_heads):
        qh = q[:, h * hq:(h + 1) * hq]
        q_ref[:, (2 * h) * hq:(2 * h + 1) * hq] = jnp.where(lane < hq // 2, qh, zero)
        q_ref[:, (2 * h + 1) * hq:(2 * h + 2) * hq] = jnp.where(lane >= hq // 2, qh, zero)
    k_ref[...] = proj(base + d_qk, d_qk).astype(k_ref.dtype)
    v_ref[...] = proj(base + 2 * d_qk, v_ref.shape[1]).astype(v_ref.dtype)


def _inproj_conv(x2, w_in_b, conv_w, conv_b, *, seq, d_conv, d_qk, d_v, qscale, tm=512):
    t, d = x2.shape
    assert seq % tm == 0
    kern = functools.partial(_inproj_kernel, tiles_per_seq=seq // tm, d_conv=d_conv, d_qk=d_qk,
                             n_heads=N_ATT_HEADS, qscale=qscale)
    return pl.pallas_call(
        kern,
        out_shape=(jax.ShapeDtypeStruct((t, d_conv), BF16), jax.ShapeDtypeStruct((t, 2 * d_qk), BF16),
                   jax.ShapeDtypeStruct((t, d_qk), BF16), jax.ShapeDtypeStruct((t, d_v), BF16)),
        grid=(t // tm,),
        in_specs=[pl.BlockSpec((tm, d), lambda i: (i, 0)),
                  pl.BlockSpec(w_in_b.shape, lambda i: (0, 0)),
                  pl.BlockSpec(conv_w.shape, lambda i: (0, 0)),
                  pl.BlockSpec(conv_b.shape, lambda i: (0, 0))],
        out_specs=(pl.BlockSpec((tm, d_conv), lambda i: (i, 0)), pl.BlockSpec((tm, 2 * d_qk), lambda i: (i, 0)),
                   pl.BlockSpec((tm, d_qk), lambda i: (i, 0)), pl.BlockSpec((tm, d_v), lambda i: (i, 0))),
        scratch_shapes=[pltpu.VMEM((SUBLANES, d_conv), F32)],
        compiler_params=pltpu.CompilerParams(dimension_semantics=("arbitrary",),
                                             vmem_limit_bytes=VMEM_LIMIT_BYTES),
        name="inproj_conv",
    )(x2, w_in_b, conv_w, conv_b)


def _attn_kernel(lq1_ref, lk1_ref, lq2_ref, lk2_ref, g_ref, q_ref, k_ref, v_ref, o_ref,
                 m_ref, l_ref, acc_ref, *, tq, n_heads, lam_init):
    qi = pl.program_id(1)
    hw = k_ref.shape[1] // n_heads
    n_chains = 2 * n_heads

    m_ref[...] = jnp.full(m_ref.shape, NEG_BIG, F32)
    l_ref[...] = jnp.zeros(l_ref.shape, F32)
    acc_ref[...] = jnp.zeros(acc_ref.shape, F32)

    def block(j, diagonal):
        off = pl.multiple_of(j * tq, tq)
        n_slab = tq // LANES
        scores = [_nt_dot(q_ref[:, c * hw:(c + 1) * hw], k_ref[pl.ds(off, tq), (c // 2) * hw:(c // 2 + 1) * hw])
                  for c in range(n_chains)]
        probs, alphas = [], []
        for c in range(n_chains):
            slabs = [scores[c][:, t * LANES:(t + 1) * LANES] for t in range(n_slab)]
            if diagonal:
                r = lax.broadcasted_iota(jnp.int32, (tq, LANES), 0)
                col = lax.broadcasted_iota(jnp.int32, (tq, LANES), 1)
                slabs = [jnp.where(col + t * LANES <= r, sl, NEG_BIG) for t, sl in enumerate(slabs)]
            mx = slabs[0]
            for sl in slabs[1:]:
                mx = jnp.maximum(mx, sl)
            m_old = m_ref[c]
            m_new = jnp.maximum(m_old, jnp.broadcast_to(jnp.max(mx, axis=1, keepdims=True), m_old.shape))
            alpha = jnp.exp2(m_old - m_new)
            ps = [jnp.exp2(sl - m_new) for sl in slabs]
            part = ps[0]
            for p in ps[1:]:
                part = part + p
            m_ref[c] = m_new
            l_ref[c] = alpha * l_ref[c] + part
            probs.append(jnp.concatenate(ps, axis=1).astype(v_ref.dtype))
            alphas.append(alpha)
        for c in range(n_chains):
            vc = v_ref[pl.ds(off, tq), (c // 2) * hw:(c // 2 + 1) * hw]
            acc_ref[c] = alphas[c] * acc_ref[c] + jnp.dot(probs[c], vc, preferred_element_type=F32)

    def body(j, carry):
        block(j, False)
        return carry

    lax.fori_loop(0, qi, body, 0)
    block(qi, True)

    lam = (jnp.exp(jnp.sum(lq1_ref[...] * lk1_ref[...], axis=1, keepdims=True))
           - jnp.exp(jnp.sum(lq2_ref[...] * lk2_ref[...], axis=1, keepdims=True)) + lam_init)
    for h in range(n_heads):
        o1 = acc_ref[2 * h] / jnp.sum(l_ref[2 * h], axis=1, keepdims=True)
        o2 = acc_ref[2 * h + 1] / jnp.sum(l_ref[2 * h + 1], axis=1, keepdims=True)
        y = o1 - lam * o2
        y = y * lax.rsqrt(jnp.mean(y * y, axis=1, keepdims=True) + LN_EPS) * g_ref[...]
        o_ref[:, h * hw:(h + 1) * hw] = (y * (1.0 - lam_init)).astype(o_ref.dtype)


def _diff_attn(q, k, v, lam_q1, lam_k1, lam_q2, lam_k2, subln_g, *, batch, seq, lam_init, tq=256):
    t, d_k = k.shape
    d_v = v.shape[1]
    n_heads = N_ATT_HEADS
    assert d_k == d_v and q.shape[1] == 2 * d_k
    nq = seq // tq
    kern = functools.partial(_attn_kernel, tq=tq, n_heads=n_heads, lam_init=lam_init)
    small = lambda a: pl.BlockSpec(a.shape, lambda b, i: (0, 0))
    n_chains = 2 * n_heads
    return pl.pallas_call(
        kern,
        out_shape=jax.ShapeDtypeStruct((t, d_v), BF16),
        grid=(batch, nq),
        in_specs=[small(lam_q1), small(lam_k1), small(lam_q2), small(lam_k2), small(subln_g),
                  pl.BlockSpec((tq, 2 * d_k), lambda b, i: (b * nq + i, 0)),
                  pl.BlockSpec((seq, d_k), lambda b, i: (b, 0)),
                  pl.BlockSpec((seq, d_v), lambda b, i: (b, 0))],
        out_specs=pl.BlockSpec((tq, d_v), lambda b, i: (b * nq + i, 0)),
        scratch_shapes=[pltpu.VMEM((n_chains, tq, LANES), F32), pltpu.VMEM((n_chains, tq, LANES), F32),
                        pltpu.VMEM((n_chains, tq, d_v // n_heads), F32)],
        compiler_params=pltpu.CompilerParams(dimension_semantics=("arbitrary", "arbitrary"),
                                             vmem_limit_bytes=VMEM_LIMIT_BYTES),
        name="diff_attn",
    )(lam_q1, lam_k1, lam_q2, lam_k2, subln_g, q, k, v)


def _layer_norm(z, g, b):
    mu = jnp.mean(z, axis=1, keepdims=True)
    zc = z - mu
    var = jnp.mean(zc * zc, axis=1, keepdims=True)
    return zc * lax.rsqrt(var + LN_EPS) * g + b


def _outproj_kernel(x_ref, yc_ref, ya_ref, wo_ref, g_ref, b_ref, wq_ref, keys_ref,
                    x1_ref, x1b_ref, sc_ref, *, alpha, d_conv):
    ymix = (jnp.dot(yc_ref[...], wo_ref[:d_conv, :], preferred_element_type=F32)
            + jnp.dot(ya_ref[...], wo_ref[d_conv:, :], preferred_element_type=F32))
    x1 = _layer_norm(alpha * x_ref[...] + ymix, g_ref[...], b_ref[...])
    x1_ref[...] = x1
    x1b = x1.astype(BF16)
    x1b_ref[...] = x1b
    n_sub, _, dh = keys_ref.shape
    for s in range(n_sub):
        qs = jnp.dot(x1b, wq_ref[:, s * dh:(s + 1) * dh], preferred_element_type=F32).astype(BF16)
        sc_ref[s] = _nt_dot(keys_ref[s], qs)


def _outproj_query(x2, yconv, yatt, w_out_b, ln_g, ln_b, wq_b, keys_b, *, alpha, tm=512):
    t, d = x2.shape
    d_conv = yconv.shape[1]
    n_sub, nkeys, _ = keys_b.shape
    kern = functools.partial(_outproj_kernel, alpha=alpha, d_conv=d_conv)
    full = lambda a: pl.BlockSpec(a.shape, lambda i: (0,) * a.ndim)
    return pl.pallas_call(
        kern,
        out_shape=(jax.ShapeDtypeStruct((t, d), F32), jax.ShapeDtypeStruct((t, d), BF16),
                   jax.ShapeDtypeStruct((n_sub, nkeys, t), F32)),
        grid=(t // tm,),
        in_specs=[pl.BlockSpec((tm, d), lambda i: (i, 0)),
                  pl.BlockSpec((tm, d_conv), lambda i: (i, 0)),
                  pl.BlockSpec((tm, yatt.shape[1]), lambda i: (i, 0)),
                  full(w_out_b), full(ln_g), full(ln_b), full(wq_b), full(keys_b)],
        out_specs=(pl.BlockSpec((tm, d), lambda i: (i, 0)), pl.BlockSpec((tm, d), lambda i: (i, 0)),
                   pl.BlockSpec((n_sub, nkeys, tm), lambda i: (0, 0, i))),
        compiler_params=pltpu.CompilerParams(dimension_semantics=("arbitrary",),
                                             vmem_limit_bytes=VMEM_LIMIT_BYTES),
        name="outproj_query",
    )(x2, yconv, yatt, w_out_b, ln_g, ln_b, wq_b, keys_b)


def _oddeven_merge_sort_pairs(n):
    pairs = []
    p = 1
    while p < n:
        k = p
        while k >= 1:
            for j in range(k % p, n - k, 2 * k):
                for i in range(min(k, n - j - k)):
                    if (i + j) // (2 * p) == (i + j + k) // (2 * p):
                        pairs.append((i + j, i + j + k))
            k //= 2
        p *= 2
    return pairs


def _sort_desc(vals):
    vals = list(vals)
    for i, j in _oddeven_merge_sort_pairs(len(vals)):
        hi = jnp.maximum(vals[i], vals[j])
        lo = jnp.minimum(vals[i], vals[j])
        vals[i], vals[j] = hi, lo
    return vals


def _bitonic_merge_desc(vals):
    vals = list(vals)
    n = len(vals)
    d = n // 2
    while d >= 1:
        for i in range(n):
            if i & d == 0:
                hi = jnp.maximum(vals[i], vals[i + d])
                lo = jnp.minimum(vals[i], vals[i + d])
                vals[i], vals[i + d] = hi, lo
        d //= 2
    return vals


def _top_sorted(s, k):
    n = s.shape[0]
    assert n == k * SUBLANES
    slabs = _sort_desc([s[a * SUBLANES:(a + 1) * SUBLANES, :] for a in range(k)])
    shift = SUBLANES // 2
    while shift >= 1:
        other = [pltpu.roll(x, shift, 0) for x in slabs]
        slabs = _bitonic_merge_desc([jnp.maximum(slabs[r], other[k - 1 - r]) for r in range(k)])
        shift //= 2
    return [x[0:1, :] for x in slabs]


def _next_below(s, kth, k):
    ge = s >= kth
    cnt = jnp.sum(jnp.where(ge, 1.0, 0.0), axis=0, keepdims=True)
    below = jnp.max(jnp.where(ge, NEG_BIG, s), axis=0, keepdims=True)
    return jnp.where(cnt > k, kth, below)


def _gate_kernel(sc_ref, w1_ref, nsel_ref, w2_ref, rank_ref, *, topk):
    n_sub = sc_ref.shape[0]
    heads = n_sub // 2
    k = topk
    tops = []
    for s_idx in range(n_sub):
        s = sc_ref[s_idx]
        rows = _top_sorted(s, k)
        rows.append(_next_below(s, rows[k - 1], k))
        tops.append([jnp.exp(r - rows[0]) for r in rows])
    a = [jnp.concatenate([tops[2 * h][r] for h in range(heads)], axis=0) for r in range(k + 1)]
    b = [jnp.concatenate([tops[2 * h + 1][r] for h in range(heads)], axis=0) for r in range(k + 1)]
    cands = [a[r] * b[c] for r in range(k + 1) for c in range(k + 1) if (r + 1) * (c + 1) <= k + 1]
    n_pad = 1
    while n_pad < len(cands):
        n_pad *= 2
    pad = jnp.full_like(cands[0], -1.0)
    srt = _sort_desc(cands + [pad] * (n_pad - len(cands)))
    tstar = 0.5 * (srt[k - 1] + srt[k])
    z = jnp.zeros_like(tstar)
    for c in cands:
        z = z + jnp.where(c >= tstar, c, 0.0)
    zinv = 1.0 / z
    for h in range(heads):
        e1 = jnp.exp(sc_ref[2 * h] - jnp.max(sc_ref[2 * h], axis=0, keepdims=True))
        e2 = jnp.exp(sc_ref[2 * h + 1] - jnp.max(sc_ref[2 * h + 1], axis=0, keepdims=True))
        theta = tstar[h:h + 1, :] / e1
        rank = jnp.zeros_like(e2)
        nsel = jnp.zeros_like(e1)
        for r in range(k):
            top_r = tops[2 * h + 1][r]
            rank = rank + jnp.where(e2 < top_r, 1.0, 0.0)
            nsel = nsel + jnp.where(top_r >= theta, 1.0, 0.0)
        w1_ref[h] = e1
        nsel_ref[h] = nsel
        w2_ref[h] = (e2 * zinv[h:h + 1, :]).astype(w2_ref.dtype)
        rank_ref[h] = rank.astype(rank_ref.dtype)


def _peer_gate(sc, *, topk, tl=256):
    n_sub, nkeys, t = sc.shape
    heads = n_sub // 2
    out_f32 = jax.ShapeDtypeStruct((heads, nkeys, t), F32)
    out_b16 = jax.ShapeDtypeStruct((heads, nkeys, t), BF16)
    spec = pl.BlockSpec((heads, nkeys, tl), lambda i: (0, 0, i))
    return pl.pallas_call(
        functools.partial(_gate_kernel, topk=topk),
        out_shape=(out_f32, out_f32, out_b16, out_b16),
        grid=(t // tl,),
        in_specs=[pl.BlockSpec((n_sub, nkeys, tl), lambda i: (0, 0, i))],
        out_specs=(spec, spec, spec, spec),
        compiler_params=pltpu.CompilerParams(dimension_semantics=("arbitrary",),
                                             vmem_limit_bytes=VMEM_LIMIT_BYTES),
        name="peer_gate",
    )(sc)


def _peer_kernel(x1b_ref, x1_ref, u_ref, vt_ref, w1_ref, nsel_ref, w2_ref, rank_ref, g_ref, b_ref, o_ref,
                 acc_ref, act0_ref, act1_ref, ga0_ref, ga1_ref, w2s_ref, ranks_ref,
                 *, alpha, nkeys, n_eblk, piece):
    g = pl.program_id(0)
    heads = w2_ref.shape[0]
    n_i = w1_ref.shape[1]
    tm = x1b_ref.shape[0]
    n_lt = tm // LANES

    @pl.when(g == 0)
    def _():
        for ref in (acc_ref, act0_ref, act1_ref, ga0_ref, ga1_ref, w2s_ref, ranks_ref):
            ref[...] = jnp.zeros_like(ref)

    @pl.when(jnp.logical_and(g >= 1, (g - 1) % n_eblk == 0))
    def _():
        for c in range(n_lt):
            w2s_ref[:, c] = w2_ref[:, :, c * LANES:(c + 1) * LANES]
            ranks_ref[:, c] = rank_ref[:, :, c * LANES:(c + 1) * LANES]

    def chunk(il, c, act_ref, ga_ref):
        lanes = pl.ds(c * LANES, LANES)
        rows = pl.ds(il * nkeys, nkeys)
        gate = jnp.zeros((nkeys, LANES), BF16)
        for h in range(heads):
            rank = ranks_ref[h, c]
            w2 = w2s_ref[h, c]
            nsel = nsel_ref[h, il:il + 1, lanes].astype(BF16)
            w1 = w1_ref[h, il:il + 1, lanes].astype(BF16)
            gate = gate + jnp.where(rank < nsel, w2, jnp.zeros_like(w2)) * w1
        a = act_ref[c, rows, :]
        gelu = 0.5 * a * (1.0 + lax.erf(a * math.sqrt(0.5)))
        ga_ref[rows, lanes] = gate * gelu.astype(BF16)

    def stages(act_w, act_r, ga_w, ga_r):
        be, d = u_ref.shape
        assert be == d
        chunks = [(il, c) for il in range(n_i) for c in range(n_lt)]
        tiles = [(r, n) for r in range(be // piece) for n in range(tm // piece)]
        per = len(chunks) // (2 * len(tiles))
        assert per * 2 * len(tiles) == len(chunks)
        sub = piece // LANES
        todo = iter(chunks)
        for r, n in tiles:
            rows = pl.ds(r * piece, piece)
            cols = pl.ds(n * piece, piece)
            res = _nt_dot(u_ref[rows, :], x1b_ref[cols, :])
            for s in range(sub):
                act_w[n * sub + s, rows, :] = res[:, s * LANES:(s + 1) * LANES]
            for _ in range(per):
                chunk(*next(todo), act_r, ga_w)
            acc_ref[rows, cols] += jnp.dot(vt_ref[0, rows, :], ga_r[:, cols],
                                           preferred_element_type=F32)
            for _ in range(per):
                chunk(*next(todo), act_r, ga_w)

    @pl.when(g % 2 == 0)
    def _():
        stages(act0_ref, act1_ref, ga1_ref, ga0_ref)

    @pl.when(g % 2 == 1)
    def _():
        stages(act1_ref, act0_ref, ga0_ref, ga1_ref)

    @pl.when(jnp.logical_and(g >= 2, (g - 2) % n_eblk == n_eblk - 1))
    def _():
        z = alpha * x1_ref[...] + acc_ref[...].T
        o_ref[...] = _layer_norm(z, g_ref[...], b_ref[...])
        acc_ref[...] = jnp.zeros_like(acc_ref)


def _peer_dense(x1b, x1, u_b, vt_blk, w1, nsel, w2, rank, ln_g, ln_b, *, alpha, tm=1024, piece=256):
    t, d = x1.shape
    n_eblk, _, be = vt_blk.shape
    heads, nkeys, _ = w2.shape
    n_i = be // nkeys
    n_items = (t // tm) * n_eblk
    last = n_items - 1
    item_a = lambda g: jnp.minimum(g, last)
    item_b = lambda g: jnp.clip(g - 1, 0, last)
    item_c = lambda g: jnp.clip(g - 2, 0, last)
    once = pl.Buffered(1)
    kern = functools.partial(_peer_kernel, alpha=alpha, nkeys=nkeys, n_eblk=n_eblk, piece=piece)
    n_lt = tm // LANES
    return pl.pallas_call(
        kern,
        out_shape=jax.ShapeDtypeStruct((t, d), F32),
        grid=(n_items + 2,),
        in_specs=[pl.BlockSpec((tm, d), lambda g: (item_a(g) // n_eblk, 0)),
                  pl.BlockSpec((tm, d), lambda g: (item_c(g) // n_eblk, 0), pipeline_mode=once),
                  pl.BlockSpec((be, d), lambda g: (item_a(g) % n_eblk, 0)),
                  pl.BlockSpec((1, d, be), lambda g: (item_c(g) % n_eblk, 0, 0)),
                  pl.BlockSpec((heads, n_i, tm), lambda g: (0, item_b(g) % n_eblk, item_b(g) // n_eblk)),
                  pl.BlockSpec((heads, n_i, tm), lambda g: (0, item_b(g) % n_eblk, item_b(g) // n_eblk)),
                  pl.BlockSpec((heads, nkeys, tm), lambda g: (0, 0, item_b(g) // n_eblk), pipeline_mode=once),
                  pl.BlockSpec((heads, nkeys, tm), lambda g: (0, 0, item_b(g) // n_eblk), pipeline_mode=once),
                  pl.BlockSpec(ln_g.shape, lambda g: (0, 0)),
                  pl.BlockSpec(ln_b.shape, lambda g: (0, 0))],
        out_specs=pl.BlockSpec((tm, d), lambda g: (item_c(g) // n_eblk, 0), pipeline_mode=once),
        scratch_shapes=[pltpu.VMEM((d, tm), F32),
                        pltpu.VMEM((n_lt, be, LANES), F32), pltpu.VMEM((n_lt, be, LANES), F32),
                        pltpu.VMEM((be, tm), BF16), pltpu.VMEM((be, tm), BF16),
                        pltpu.VMEM((heads, n_lt, nkeys, LANES), BF16),
                        pltpu.VMEM((heads, n_lt, nkeys, LANES), BF16)],
        compiler_params=pltpu.CompilerParams(dimension_semantics=("arbitrary",),
                                             vmem_limit_bytes=VMEM_LIMIT_BYTES),
        name="peer_dense",
    )(x1b, x1, u_b, vt_blk, w1, nsel, w2, rank, ln_g, ln_b)


def kernel(x, w_in, conv_w, conv_b, lam_q1, lam_k1, lam_q2, lam_k2, subln_g, w_out, ln1_g, ln1_b,
           peer_wq, peer_keys, peer_u, peer_v, ln2_g, ln2_b):
    batch, seq, d = x.shape
    depth = w_in.shape[0]
    d_conv = conv_w.shape[2]
    att_vdim = subln_g.shape[1]
    d_v = N_ATT_HEADS * att_vdim
    d_qk = (w_in.shape[2] - 3 * d_conv - d_v) // 2
    qk_dim = d_qk // (2 * N_ATT_HEADS)
    alpha = (2 * depth) ** 0.25
    heads, _, nkeys, dhalf = peer_keys.shape[1:]
    row = lambda a: a.reshape(1, -1)

    x2 = x.reshape(batch * seq, d)
    for l in range(depth):
        lam_init = 0.8 - 0.6 * math.exp(-0.3 * l)
        qscale = LOG2E / math.sqrt(qk_dim)
        yconv, q, k, v = _inproj_conv(x2, w_in[l].astype(BF16), conv_w[l], row(conv_b[l]), seq=seq,
                                      d_conv=d_conv, d_qk=d_qk, d_v=d_v, qscale=qscale)
        yatt = _diff_attn(q, k, v, row(lam_q1[l]), row(lam_k1[l]), row(lam_q2[l]), row(lam_k2[l]),
                          row(subln_g[l]), batch=batch, seq=seq, lam_init=lam_init)
        x1, x1b, sc = _outproj_query(x2, yconv, yatt, w_out[l].astype(BF16), row(ln1_g[l]), row(ln1_b[l]),
                                     peer_wq[l].astype(BF16),
                                     peer_keys[l].reshape(heads * 2, nkeys, dhalf).astype(BF16), alpha=alpha)
        w1, nsel, w2, rank = _peer_gate(sc, topk=PEER_TOPK)
        vt_blk = peer_v[l].reshape(-1, PEER_EXPERT_BLOCK, d).transpose(0, 2, 1).astype(BF16)
        x2 = _peer_dense(x1b, x1, peer_u[l].astype(BF16), vt_blk, w1, nsel, w2, rank,
                         row(ln2_g[l]), row(ln2_b[l]), alpha=alpha)
    return x2.reshape(batch, seq, d)
```

```python
import functools
import math

import jax
import jax.numpy as jnp
from jax import lax
from jax.experimental import pallas as pl
from jax.experimental.pallas import tpu as pltpu

F32 = jnp.float32
BF16 = jnp.bfloat16

CONV_WIDTH = 3
N_ATT_HEADS = 4
PEER_HEADS = 8
PEER_NKEYS = 128
PEER_TOPK = 16
PEER_EXPERT_BLOCK = 1024
CHUNK_KEYS = 2
LN_EPS = 1e-5
LOG2E = 1.4426950408889634
NEG_BIG = -1e30

SUBLANES = 8
LANES = 128
VMEM_LIMIT_BYTES = 56 * 1024 * 1024


def _nt_dot(a, b):
    return lax.dot_general(a, b, (((1,), (1,)), ((), ())), preferred_element_type=F32)


def _inproj_kernel(x_ref, w_ref, cw_ref, cb_ref, yconv_ref, q_ref, k_ref, v_ref, carry_ref,
                   *, tiles_per_seq, d_conv, d_qk, n_heads, qscale):
    i = pl.program_id(0)

    @pl.when(i % tiles_per_seq == 0)
    def _():
        carry_ref[...] = jnp.zeros_like(carry_ref)

    xb = x_ref[...].astype(BF16)

    def proj(lo, width):
        return jnp.dot(xb, w_ref[:, lo:lo + width], preferred_element_type=F32)

    gb = proj(0, d_conv)
    u = proj(d_conv, d_conv) * proj(2 * d_conv, d_conv)
    tm = u.shape[0]
    row = lax.broadcasted_iota(jnp.int32, u.shape, 0)
    c1 = carry_ref[SUBLANES - 1:SUBLANES, :]
    c2 = carry_ref[SUBLANES - 2:SUBLANES - 1, :]
    u1 = jnp.where(row == 0, c1, pltpu.roll(u, 1, 0))
    u2 = jnp.where(row == 0, c2, jnp.where(row == 1, c1, pltpu.roll(u, 2, 0)))
    conv = cb_ref[...] + cw_ref[2:3, :] * u + cw_ref[1:2, :] * u1 + cw_ref[0:1, :] * u2
    carry_ref[...] = u[tm - SUBLANES:, :]
    yconv_ref[...] = (gb * conv).astype(yconv_ref.dtype)

    base = 3 * d_conv
    q = (proj(base, d_qk) * qscale).astype(q_ref.dtype)
    hq = d_qk // n_heads
    lane = lax.broadcasted_iota(jnp.int32, (tm, hq), 1)
    zero = jnp.zeros((tm, hq), q_ref.dtype)
    for h in range(n_heads):
        qh = q[:, h * hq:(h + 1) * hq]
        q_ref[:, (2 * h) * hq:(2 * h + 1) * hq] = jnp.where(lane < hq // 2, qh, zero)
        q_ref[:, (2 * h + 1) * hq:(2 * h + 2) * hq] = jnp.where(lane >= hq // 2, qh, zero)
    k_ref[...] = proj(base + d_qk, d_qk).astype(k_ref.dtype)
    v_ref[...] = proj(base + 2 * d_qk, v_ref.shape[1]).astype(v_ref.dtype)


def _inproj_conv(x2, w_in_b, conv_w, conv_b, *, seq, d_conv, d_qk, d_v, qscale, tm=512):
    t, d = x2.shape
    assert seq % tm == 0
    kern = functools.partial(_inproj_kernel, tiles_per_seq=seq // tm, d_conv=d_conv, d_qk=d_qk,
                             n_heads=N_ATT_HEADS, qscale=qscale)
    return pl.pallas_call(
        kern,
        out_shape=(jax.ShapeDtypeStruct((t, d_conv), BF16), jax.ShapeDtypeStruct((t, 2 * d_qk), BF16),
                   jax.ShapeDtypeStruct((t, d_qk), BF16), jax.ShapeDtypeStruct((t, d_v), BF16)),
        grid=(t // tm,),
        in_specs=[pl.BlockSpec((tm, d), lambda i: (i, 0)),
                  pl.BlockSpec(w_in_b.shape, lambda i: (0, 0)),
                  pl.BlockSpec(conv_w.shape, lambda i: (0, 0)),
                  pl.BlockSpec(conv_b.shape, lambda i: (0, 0))],
        out_specs=(pl.BlockSpec((tm, d_conv), lambda i: (i, 0)), pl.BlockSpec((tm, 2 * d_qk), lambda i: (i, 0)),
                   pl.BlockSpec((tm, d_qk), lambda i: (i, 0)), pl.BlockSpec((tm, d_v), lambda i: (i, 0))),
        scratch_shapes=[pltpu.VMEM((SUBLANES, d_conv), F32)],
        compiler_params=pltpu.CompilerParams(dimension_semantics=("arbitrary",),
                                             vmem_limit_bytes=VMEM_LIMIT_BYTES),
        name="inproj_conv",
    )(x2, w_in_b, conv_w, conv_b)


def _attn_kernel(lq1_ref, lk1_ref, lq2_ref, lk2_ref, g_ref, q_ref, k_ref, v_ref, o_ref,
                 m_ref, l_ref, acc_ref, *, tq, n_heads, lam_init):
    qi = pl.program_id(1)
    hw = k_ref.shape[1] // n_heads
    n_chains = 2 * n_heads

    m_ref[...] = jnp.full(m_ref.shape, NEG_BIG, F32)
    l_ref[...] = jnp.zeros(l_ref.shape, F32)
    acc_ref[...] = jnp.zeros(acc_ref.shape, F32)

    def block(j, diagonal):
        off = pl.multiple_of(j * tq, tq)
        n_slab = tq // LANES
        scores = [_nt_dot(q_ref[:, c * hw:(c + 1) * hw], k_ref[pl.ds(off, tq), (c // 2) * hw:(c // 2 + 1) * hw])
                  for c in range(n_chains)]
        probs, alphas = [], []
        for c in range(n_chains):
            slabs = [scores[c][:, t * LANES:(t + 1) * LANES] for t in range(n_slab)]
            if diagonal:
                r = lax.broadcasted_iota(jnp.int32, (tq, LANES), 0)
                col = lax.broadcasted_iota(jnp.int32, (tq, LANES), 1)
                slabs = [jnp.where(col + t * LANES <= r, sl, NEG_BIG) for t, sl in enumerate(slabs)]
            mx = slabs[0]
            for sl in slabs[1:]:
                mx = jnp.maximum(mx, sl)
            m_old = m_ref[c]
            m_new = jnp.maximum(m_old, jnp.broadcast_to(jnp.max(mx, axis=1, keepdims=True), m_old.shape))
            alpha = jnp.exp2(m_old - m_new)
            ps = [jnp.exp2(sl - m_new) for sl in slabs]
            part = ps[0]
            for p in ps[1:]:
                part = part + p
            m_ref[c] = m_new
            l_ref[c] = alpha * l_ref[c] + part
            probs.append(jnp.concatenate(ps, axis=1).astype(v_ref.dtype))
            alphas.append(alpha)
        for c in range(n_chains):
            vc = v_ref[pl.ds(off, tq), (c // 2) * hw:(c // 2 + 1) * hw]
            acc_ref[c] = alphas[c] * acc_ref[c] + jnp.dot(probs[c], vc, preferred_element_type=F32)

    def body(j, carry):
        block(j, False)
        return carry

    lax.fori_loop(0, qi, body, 0)
    block(qi, True)

    lam = (jnp.exp(jnp.sum(lq1_ref[...] * lk1_ref[...], axis=1, keepdims=True))
           - jnp.exp(jnp.sum(lq2_ref[...] * lk2_ref[...], axis=1, keepdims=True)) + lam_init)
    for h in range(n_heads):
        o1 = acc_ref[2 * h] / jnp.sum(l_ref[2 * h], axis=1, keepdims=True)
        o2 = acc_ref[2 * h + 1] / jnp.sum(l_ref[2 * h + 1], axis=1, keepdims=True)
        y = o1 - lam * o2
        y = y * lax.rsqrt(jnp.mean(y * y, axis=1, keepdims=True) + LN_EPS) * g_ref[...]
        o_ref[:, h * hw:(h + 1) * hw] = (y * (1.0 - lam_init)).astype(o_ref.dtype)


def _diff_attn(q, k, v, lam_q1, lam_k1, lam_q2, lam_k2, subln_g, *, batch, seq, lam_init, tq=256):
    t, d_k = k.shape
    d_v = v.shape[1]
    n_heads = N_ATT_HEADS
    assert d_k == d_v and q.shape[1] == 2 * d_k
    nq = seq // tq
    kern = functools.partial(_attn_kernel, tq=tq, n_heads=n_heads, lam_init=lam_init)
    small = lambda a: pl.BlockSpec(a.shape, lambda b, i: (0, 0))
    n_chains = 2 * n_heads
    return pl.pallas_call(
        kern,
        out_shape=jax.ShapeDtypeStruct((t, d_v), BF16),
        grid=(batch, nq),
        in_specs=[small(lam_q1), small(lam_k1), small(lam_q2), small(lam_k2), small(subln_g),
                  pl.BlockSpec((tq, 2 * d_k), lambda b, i: (b * nq + i, 0)),
                  pl.BlockSpec((seq, d_k), lambda b, i: (b, 0)),
                  pl.BlockSpec((seq, d_v), lambda b, i: (b, 0))],
        out_specs=pl.BlockSpec((tq, d_v), lambda b, i: (b * nq + i, 0)),
        scratch_shapes=[pltpu.VMEM((n_chains, tq, LANES), F32), pltpu.VMEM((n_chains, tq, LANES), F32),
                        pltpu.VMEM((n_chains, tq, d_v // n_heads), F32)],
        compiler_params=pltpu.CompilerParams(dimension_semantics=("arbitrary", "arbitrary"),
                                             vmem_limit_bytes=VMEM_LIMIT_BYTES),
        name="diff_attn",
    )(lam_q1, lam_k1, lam_q2, lam_k2, subln_g, q, k, v)


def _layer_norm(z, g, b):
    mu = jnp.mean(z, axis=1, keepdims=True)
    zc = z - mu
    var = jnp.mean(zc * zc, axis=1, keepdims=True)
    return zc * lax.rsqrt(var + LN_EPS) * g + b


def _outproj_kernel(x_ref, yc_ref, ya_ref, wo_ref, g_ref, b_ref, wq_ref, keys_ref,
                    x1_ref, x1t_ref, sc_ref, *, alpha, d_conv):
    ymix = (jnp.dot(yc_ref[...], wo_ref[:d_conv, :], preferred_element_type=F32)
            + jnp.dot(ya_ref[...], wo_ref[d_conv:, :], preferred_element_type=F32))
    x1 = _layer_norm(alpha * x_ref[...] + ymix, g_ref[...], b_ref[...])
    x1_ref[...] = x1
    x1t_ref[...] = x1.T.astype(BF16)
    n_sub, _, dh = keys_ref.shape
    q = jnp.dot(x1.astype(BF16), wq_ref[...], preferred_element_type=F32).astype(BF16)
    for s in range(n_sub):
        sc_ref[s] = _nt_dot(keys_ref[s], q[:, s * dh:(s + 1) * dh])


def _outproj_query(x2, yconv, yatt, w_out_b, ln_g, ln_b, wq_b, keys_b, *, alpha, tm=512):
    t, d = x2.shape
    d_conv = yconv.shape[1]
    n_sub, nkeys, _ = keys_b.shape
    kern = functools.partial(_outproj_kernel, alpha=alpha, d_conv=d_conv)
    full = lambda a: pl.BlockSpec(a.shape, lambda i: (0,) * a.ndim)
    return pl.pallas_call(
        kern,
        out_shape=(jax.ShapeDtypeStruct((t, d), F32), jax.ShapeDtypeStruct((d, t), BF16),
                   jax.ShapeDtypeStruct((n_sub, nkeys, t), F32)),
        grid=(t // tm,),
        in_specs=[pl.BlockSpec((tm, d), lambda i: (i, 0)),
                  pl.BlockSpec((tm, d_conv), lambda i: (i, 0)),
                  pl.BlockSpec((tm, yatt.shape[1]), lambda i: (i, 0)),
                  full(w_out_b), full(ln_g), full(ln_b), full(wq_b), full(keys_b)],
        out_specs=(pl.BlockSpec((tm, d), lambda i: (i, 0)), pl.BlockSpec((d, tm), lambda i: (0, i)),
                   pl.BlockSpec((n_sub, nkeys, tm), lambda i: (0, 0, i))),
        compiler_params=pltpu.CompilerParams(dimension_semantics=("arbitrary",),
                                             vmem_limit_bytes=VMEM_LIMIT_BYTES),
        name="outproj_query",
    )(x2, yconv, yatt, w_out_b, ln_g, ln_b, wq_b, keys_b)


def _oddeven_merge_sort_pairs(n):
    pairs = []
    p = 1
    while p < n:
        k = p
        while k >= 1:
            for j in range(k % p, n - k, 2 * k):
                for i in range(min(k, n - j - k)):
                    if (i + j) // (2 * p) == (i + j + k) // (2 * p):
                        pairs.append((i + j, i + j + k))
            k //= 2
        p *= 2
    return pairs


def _sort_desc(vals):
    vals = list(vals)
    for i, j in _oddeven_merge_sort_pairs(len(vals)):
        hi = jnp.maximum(vals[i], vals[j])
        lo = jnp.minimum(vals[i], vals[j])
        vals[i], vals[j] = hi, lo
    return vals


def _bitonic_merge_desc(vals):
    vals = list(vals)
    n = len(vals)
    d = n // 2
    while d >= 1:
        for i in range(n):
            if i & d == 0:
                hi = jnp.maximum(vals[i], vals[i + d])
                lo = jnp.minimum(vals[i], vals[i + d])
                vals[i], vals[i + d] = hi, lo
        d //= 2
    return vals


def _top_sorted(s, k):
    n = s.shape[0]
    assert n == k * SUBLANES
    slabs = _sort_desc([s[a * SUBLANES:(a + 1) * SUBLANES, :] for a in range(k)])
    shift = SUBLANES // 2
    while shift >= 1:
        other = [pltpu.roll(x, shift, 0) for x in slabs]
        slabs = _bitonic_merge_desc([jnp.maximum(slabs[r], other[k - 1 - r]) for r in range(k)])
        shift //= 2
    return [x[0:1, :] for x in slabs]


def _next_below(s, kth, k):
    ge = s >= kth
    cnt = jnp.sum(jnp.where(ge, 1.0, 0.0), axis=0, keepdims=True)
    below = jnp.max(jnp.where(ge, NEG_BIG, s), axis=0, keepdims=True)
    return jnp.where(cnt > k, kth, below)


def _gate_kernel(sc_ref, w1_ref, nsel_ref, w2_ref, rank_ref, *, topk):
    n_sub = sc_ref.shape[0]
    heads = n_sub // 2
    k = topk
    tops = []
    for s_idx in range(n_sub):
        s = sc_ref[s_idx]
        rows = _top_sorted(s, k)
        rows.append(_next_below(s, rows[k - 1], k))
        tops.append([jnp.exp(r - rows[0]) for r in rows])
    a = [jnp.concatenate([tops[2 * h][r] for h in range(heads)], axis=0) for r in range(k + 1)]
    b = [jnp.concatenate([tops[2 * h + 1][r] for h in range(heads)], axis=0) for r in range(k + 1)]
    cands = [a[r] * b[c] for r in range(k + 1) for c in range(k + 1) if (r + 1) * (c + 1) <= k + 1]
    n_pad = 1
    while n_pad < len(cands):
        n_pad *= 2
    pad = jnp.full_like(cands[0], -1.0)
    srt = _sort_desc(cands + [pad] * (n_pad - len(cands)))
    tstar = 0.5 * (srt[k - 1] + srt[k])
    z = jnp.zeros_like(tstar)
    for c in cands:
        z = z + jnp.where(c >= tstar, c, 0.0)
    zinv = 1.0 / z
    for h in range(heads):
        e1 = jnp.exp(sc_ref[2 * h] - jnp.max(sc_ref[2 * h], axis=0, keepdims=True))
        e2 = jnp.exp(sc_ref[2 * h + 1] - jnp.max(sc_ref[2 * h + 1], axis=0, keepdims=True))
        theta = tstar[h:h + 1, :] / e1
        rank = jnp.zeros_like(e2)
        nsel = jnp.zeros_like(e1)
        for r in range(k):
            top_r = tops[2 * h + 1][r]
            rank = rank + jnp.where(e2 < top_r, 1.0, 0.0)
            nsel = nsel + jnp.where(top_r >= theta, 1.0, 0.0)
        w1_ref[h] = e1
        nsel_ref[h] = nsel
        w2_ref[h] = (e2 * zinv[h:h + 1, :]).astype(w2_ref.dtype)
        rank_ref[h] = rank.astype(rank_ref.dtype)


def _peer_gate(sc, *, topk, tl=256):
    n_sub, nkeys, t = sc.shape
    heads = n_sub // 2
    out_f32 = jax.ShapeDtypeStruct((heads, nkeys, t), F32)
    out_b16 = jax.ShapeDtypeStruct((heads, nkeys, t), BF16)
    spec = pl.BlockSpec((heads, nkeys, tl), lambda i: (0, 0, i))
    return pl.pallas_call(
        functools.partial(_gate_kernel, topk=topk),
        out_shape=(out_f32, out_f32, out_b16, out_b16),
        grid=(t // tl,),
        in_specs=[pl.BlockSpec((n_sub, nkeys, tl), lambda i: (0, 0, i))],
        out_specs=(spec, spec, spec, spec),
        compiler_params=pltpu.CompilerParams(dimension_semantics=("arbitrary",),
                                             vmem_limit_bytes=VMEM_LIMIT_BYTES),
        name="peer_gate",
    )(sc)


def _peer_kernel(x1t_ref, x1_ref, u_ref, vt_ref, w1_ref, nsel_ref, w2_ref, rank_ref, g_ref, b_ref, o_ref,
                 acc_ref, act0_ref, act1_ref, ga0_ref, ga1_ref, w2s_ref, ranks_ref,
                 *, alpha, nkeys, n_eblk, piece):
    g = pl.program_id(0)
    heads = w2_ref.shape[0]
    n_i = w1_ref.shape[1]
    tm = x1t_ref.shape[1]
    n_lt = tm // LANES

    @pl.when(g == 0)
    def _():
        for ref in (acc_ref, act0_ref, act1_ref, ga0_ref, ga1_ref, w2s_ref, ranks_ref):
            ref[...] = jnp.zeros_like(ref)

    @pl.when(jnp.logical_and(g >= 1, (g - 1) % n_eblk == 0))
    def _():
        for c in range(n_lt):
            w2s_ref[:, c] = w2_ref[:, :, c * LANES:(c + 1) * LANES]
            ranks_ref[:, c] = rank_ref[:, :, c * LANES:(c + 1) * LANES]

    def chunk(ils, c, act_ref, ga_ref):
        lanes = pl.ds(c * LANES, LANES)
        gates = [jnp.zeros((nkeys, LANES), BF16) for _ in ils]
        for h in range(heads):
            rank = ranks_ref[h, c]
            w2 = w2s_ref[h, c]
            for n, il in enumerate(ils):
                nsel = nsel_ref[h, il:il + 1, lanes].astype(BF16)
                w1 = w1_ref[h, il:il + 1, lanes].astype(BF16)
                gates[n] = gates[n] + jnp.where(rank < nsel, w2, jnp.zeros_like(w2)) * w1
        for n, il in enumerate(ils):
            rows = pl.ds(il * nkeys, nkeys)
            a = act_ref[c, rows, :]
            gelu = 0.5 * a * (1.0 + lax.erf(a * math.sqrt(0.5)))
            ga_ref[rows, lanes] = gates[n] * gelu.astype(BF16)

    def stages(act_w, act_r, ga_w, ga_r):
        chunks = [(tuple(range(il, il + CHUNK_KEYS)), c)
                  for il in range(0, n_i, CHUNK_KEYS) for c in range(n_lt)]
        n_col = tm // piece
        per = len(chunks) // (2 * n_col)
        assert per * 2 * n_col == len(chunks)
        sub = piece // LANES
        todo = iter(chunks)
        for n in range(n_col):
            cols = pl.ds(n * piece, piece)
            res = jnp.dot(u_ref[...], x1t_ref[:, cols], preferred_element_type=F32)
            for s in range(sub):
                act_w[n * sub + s] = res[:, s * LANES:(s + 1) * LANES]
            for _ in range(per):
                chunk(*next(todo), act_r, ga_w)
            acc_ref[:, cols] += jnp.dot(vt_ref[0], ga_r[:, cols], preferred_element_type=F32)
            for _ in range(per):
                chunk(*next(todo), act_r, ga_w)

    @pl.when(g % 2 == 0)
    def _():
        stages(act0_ref, act1_ref, ga1_ref, ga0_ref)

    @pl.when(g % 2 == 1)
    def _():
        stages(act1_ref, act0_ref, ga0_ref, ga1_ref)

    @pl.when(jnp.logical_and(g >= 2, (g - 2) % n_eblk == n_eblk - 1))
    def _():
        z = alpha * x1_ref[...] + acc_ref[...].T
        o_ref[...] = _layer_norm(z, g_ref[...], b_ref[...])
        acc_ref[...] = jnp.zeros_like(acc_ref)


def _peer_dense(x1t, x1, u_b, vt_blk, w1, nsel, w2, rank, ln_g, ln_b, *, alpha, tm=1024, piece=256):
    t, d = x1.shape
    n_eblk, _, be = vt_blk.shape
    heads, nkeys, _ = w2.shape
    n_i = be // nkeys
    n_items = (t // tm) * n_eblk
    last = n_items - 1
    item_a = lambda g: jnp.minimum(g, last)
    item_b = lambda g: jnp.clip(g - 1, 0, last)
    item_c = lambda g: jnp.clip(g - 2, 0, last)
    once = pl.Buffered(1)
    kern = functools.partial(_peer_kernel, alpha=alpha, nkeys=nkeys, n_eblk=n_eblk, piece=piece)
    n_lt = tm // LANES
    return pl.pallas_call(
        kern,
        out_shape=jax.ShapeDtypeStruct((t, d), F32),
        grid=(n_items + 2,),
        in_specs=[pl.BlockSpec((d, tm), lambda g: (0, item_a(g) // n_eblk)),
                  pl.BlockSpec((tm, d), lambda g: (item_c(g) // n_eblk, 0), pipeline_mode=once),
                  pl.BlockSpec((be, d), lambda g: (item_a(g) % n_eblk, 0)),
                  pl.BlockSpec((1, d, be), lambda g: (item_c(g) % n_eblk, 0, 0)),
                  pl.BlockSpec((heads, n_i, tm), lambda g: (0, item_b(g) % n_eblk, item_b(g) // n_eblk)),
                  pl.BlockSpec((heads, n_i, tm), lambda g: (0, item_b(g) % n_eblk, item_b(g) // n_eblk)),
                  pl.BlockSpec((heads, nkeys, tm), lambda g: (0, 0, item_b(g) // n_eblk), pipeline_mode=once),
                  pl.BlockSpec((heads, nkeys, tm), lambda g: (0, 0, item_b(g) // n_eblk), pipeline_mode=once),
                  pl.BlockSpec(ln_g.shape, lambda g: (0, 0)),
                  pl.BlockSpec(ln_b.shape, lambda g: (0, 0))],
        out_specs=pl.BlockSpec((tm, d), lambda g: (item_c(g) // n_eblk, 0), pipeline_mode=once),
        scratch_shapes=[pltpu.VMEM((d, tm), F32),
                        pltpu.VMEM((n_lt, be, LANES), F32), pltpu.VMEM((n_lt, be, LANES), F32),
                        pltpu.VMEM((be, tm), BF16), pltpu.VMEM((be, tm), BF16),
                        pltpu.VMEM((heads, n_lt, nkeys, LANES), BF16),
                        pltpu.VMEM((heads, n_lt, nkeys, LANES), BF16)],
        compiler_params=pltpu.CompilerParams(dimension_semantics=("arbitrary",),
                                             vmem_limit_bytes=VMEM_LIMIT_BYTES),
        name="peer_dense",
    )(x1t, x1, u_b, vt_blk, w1, nsel, w2, rank, ln_g, ln_b)


def kernel(x, w_in, conv_w, conv_b, lam_q1, lam_k1, lam_q2, lam_k2, subln_g, w_out, ln1_g, ln1_b,
           peer_wq, peer_keys, peer_u, peer_v, ln2_g, ln2_b):
    batch, seq, d = x.shape
    depth = w_in.shape[0]
    d_conv = conv_w.shape[2]
    att_vdim = subln_g.shape[1]
    d_v = N_ATT_HEADS * att_vdim
    d_qk = (w_in.shape[2] - 3 * d_conv - d_v) // 2
    qk_dim = d_qk // (2 * N_ATT_HEADS)
    alpha = (2 * depth) ** 0.25
    heads, _, nkeys, dhalf = peer_keys.shape[1:]
    row = lambda a: a.reshape(1, -1)

    x2 = x.reshape(batch * seq, d)
    for l in range(depth):
        lam_init = 0.8 - 0.6 * math.exp(-0.3 * l)
        qscale = LOG2E / math.sqrt(qk_dim)
        yconv, q, k, v = _inproj_conv(x2, w_in[l].astype(BF16), conv_w[l], row(conv_b[l]), seq=seq,
                                      d_conv=d_conv, d_qk=d_qk, d_v=d_v, qscale=qscale)
        yatt = _diff_attn(q, k, v, row(lam_q1[l]), row(lam_k1[l]), row(lam_q2[l]), row(lam_k2[l]),
                          row(subln_g[l]), batch=batch, seq=seq, lam_init=lam_init)
        x1, x1t, sc = _outproj_query(x2, yconv, yatt, w_out[l].astype(BF16), row(ln1_g[l]), row(ln1_b[l]),
                                     peer_wq[l].astype(BF16),
                                     peer_keys[l].reshape(heads * 2, nkeys, dhalf).astype(BF16), alpha=alpha)
        w1, nsel, w2, rank = _peer_gate(sc, topk=PEER_TOPK)
        vt_blk = peer_v[l].reshape(-1, PEER_EXPERT_BLOCK, d).transpose(0, 2, 1).astype(BF16)
        x2 = _peer_dense(x1t, x1, peer_u[l].astype(BF16), vt_blk, w1, nsel, w2, rank,
                         row(ln2_g[l]), row(ln2_b[l]), alpha=alpha)
    return x2.reshape(batch, seq, d)
```

```python
import functools
import math

import jax
import jax.numpy as jnp
from jax import lax
from jax.experimental import pallas as pl
from jax.experimental.pallas import tpu as pltpu

F32 = jnp.float32
BF16 = jnp.bfloat16

CONV_WIDTH = 3
N_ATT_HEADS = 4
PEER_HEADS = 8
PEER_NKEYS = 128
PEER_TOPK = 16
PEER_EXPERT_BLOCK = 1024
CHUNK_KEYS = 2
LN_EPS = 1e-5
LOG2E = 1.4426950408889634
NEG_BIG = -1e30

SUBLANES = 8
LANES = 128
VMEM_LIMIT_BYTES = 56 * 1024 * 1024


def _nt_dot(a, b):
    return lax.dot_general(a, b, (((1,), (1,)), ((), ())), preferred_element_type=F32)


def _inproj_kernel(x_ref, w_ref, cw_ref, cb_ref, yconv_ref, q_ref, k_ref, v_ref, carry_ref,
                   *, tiles_per_seq, d_conv, d_qk, n_heads, qscale):
    i = pl.program_id(0)

    @pl.when(i % tiles_per_seq == 0)
    def _():
        carry_ref[...] = jnp.zeros_like(carry_ref)

    xb = x_ref[...].astype(BF16)

    def proj(lo, width):
        return jnp.dot(xb, w_ref[:, lo:lo + width], preferred_element_type=F32)

    gb = proj(0, d_conv)
    u = proj(d_conv, d_conv) * proj(2 * d_conv, d_conv)
    tm = u.shape[0]
    row = lax.broadcasted_iota(jnp.int32, u.shape, 0)
    c1 = carry_ref[SUBLANES - 1:SUBLANES, :]
    c2 = carry_ref[SUBLANES - 2:SUBLANES - 1, :]
    u1 = jnp.where(row == 0, c1, pltpu.roll(u, 1, 0))
    u2 = jnp.where(row == 0, c2, jnp.where(row == 1, c1, pltpu.roll(u, 2, 0)))
    conv = cb_ref[...] + cw_ref[2:3, :] * u + cw_ref[1:2, :] * u1 + cw_ref[0:1, :] * u2
    carry_ref[...] = u[tm - SUBLANES:, :]
    yconv_ref[...] = (gb * conv).astype(yconv_ref.dtype)

    base = 3 * d_conv
    q = (proj(base, d_qk) * qscale).astype(q_ref.dtype)
    hq = d_qk // n_heads
    lane = lax.broadcasted_iota(jnp.int32, (tm, hq), 1)
    zero = jnp.zeros((tm, hq), q_ref.dtype)
    for h in range(n_heads):
        qh = q[:, h * hq:(h + 1) * hq]
        q_ref[:, (2 * h) * hq:(2 * h + 1) * hq] = jnp.where(lane < hq // 2, qh, zero)
        q_ref[:, (2 * h + 1) * hq:(2 * h + 2) * hq] = jnp.where(lane >= hq // 2, qh, zero)
    k_ref[...] = proj(base + d_qk, d_qk).astype(k_ref.dtype)
    v_ref[...] = proj(base + 2 * d_qk, v_ref.shape[1]).astype(v_ref.dtype)


def _inproj_conv(x2, w_in_b, conv_w, conv_b, *, seq, d_conv, d_qk, d_v, qscale, tm=512):
    t, d = x2.shape
    assert seq % tm == 0
    kern = functools.partial(_inproj_kernel, tiles_per_seq=seq // tm, d_conv=d_conv, d_qk=d_qk,
                             n_heads=N_ATT_HEADS, qscale=qscale)
    return pl.pallas_call(
        kern,
        out_shape=(jax.ShapeDtypeStruct((t, d_conv), BF16), jax.ShapeDtypeStruct((t, 2 * d_qk), BF16),
                   jax.ShapeDtypeStruct((t, d_qk), BF16), jax.ShapeDtypeStruct((t, d_v), BF16)),
        grid=(t // tm,),
        in_specs=[pl.BlockSpec((tm, d), lambda i: (i, 0)),
                  pl.BlockSpec(w_in_b.shape, lambda i: (0, 0)),
                  pl.BlockSpec(conv_w.shape, lambda i: (0, 0)),
                  pl.BlockSpec(conv_b.shape, lambda i: (0, 0))],
        out_specs=(pl.BlockSpec((tm, d_conv), lambda i: (i, 0)), pl.BlockSpec((tm, 2 * d_qk), lambda i: (i, 0)),
                   pl.BlockSpec((tm, d_qk), lambda i: (i, 0)), pl.BlockSpec((tm, d_v), lambda i: (i, 0))),
        scratch_shapes=[pltpu.VMEM((SUBLANES, d_conv), F32)],
        compiler_params=pltpu.CompilerParams(dimension_semantics=("arbitrary",),
                                             vmem_limit_bytes=VMEM_LIMIT_BYTES),
        name="inproj_conv",
    )(x2, w_in_b, conv_w, conv_b)


def _attn_kernel(lq1_ref, lk1_ref, lq2_ref, lk2_ref, g_ref, q_ref, k_ref, v_ref, o_ref,
                 m_ref, l_ref, acc_ref, *, tq, n_heads, lam_init):
    qi = pl.program_id(1)
    hw = k_ref.shape[1] // n_heads
    n_chains = 2 * n_heads

    m_ref[...] = jnp.full(m_ref.shape, NEG_BIG, F32)
    l_ref[...] = jnp.zeros(l_ref.shape, F32)
    acc_ref[...] = jnp.zeros(acc_ref.shape, F32)

    def block(j, diagonal):
        off = pl.multiple_of(j * tq, tq)
        n_slab = tq // LANES
        scores = [_nt_dot(q_ref[:, c * hw:(c + 1) * hw], k_ref[pl.ds(off, tq), (c // 2) * hw:(c // 2 + 1) * hw])
                  for c in range(n_chains)]
        probs, alphas = [], []
        for c in range(n_chains):
            slabs = [scores[c][:, t * LANES:(t + 1) * LANES] for t in range(n_slab)]
            if diagonal:
                r = lax.broadcasted_iota(jnp.int32, (tq, LANES), 0)
                col = lax.broadcasted_iota(jnp.int32, (tq, LANES), 1)
                slabs = [jnp.where(col + t * LANES <= r, sl, NEG_BIG) for t, sl in enumerate(slabs)]
            mx = slabs[0]
            for sl in slabs[1:]:
                mx = jnp.maximum(mx, sl)
            m_old = m_ref[c]
            m_new = jnp.maximum(m_old, jnp.broadcast_to(jnp.max(mx, axis=1, keepdims=True), m_old.shape))
            alpha = jnp.exp2(m_old - m_new)
            ps = [jnp.exp2(sl - m_new) for sl in slabs]
            part = ps[0]
            for p in ps[1:]:
                part = part + p
            m_ref[c] = m_new
            l_ref[c] = alpha * l_ref[c] + part
            probs.append(jnp.concatenate(ps, axis=1).astype(v_ref.dtype))
            alphas.append(alpha)
        for c in range(n_chains):
            vc = v_ref[pl.ds(off, tq), (c // 2) * hw:(c // 2 + 1) * hw]
            acc_ref[c] = alphas[c] * acc_ref[c] + jnp.dot(probs[c], vc, preferred_element_type=F32)

    def body(j, carry):
        block(j, False)
        return carry

    lax.fori_loop(0, qi, body, 0)
    block(qi, True)

    lam = (jnp.exp(jnp.sum(lq1_ref[...] * lk1_ref[...], axis=1, keepdims=True))
           - jnp.exp(jnp.sum(lq2_ref[...] * lk2_ref[...], axis=1, keepdims=True)) + lam_init)
    for h in range(n_heads):
        o1 = acc_ref[2 * h] / jnp.sum(l_ref[2 * h], axis=1, keepdims=True)
        o2 = acc_ref[2 * h + 1] / jnp.sum(l_ref[2 * h + 1], axis=1, keepdims=True)
        y = o1 - lam * o2
        y = y * lax.rsqrt(jnp.mean(y * y, axis=1, keepdims=True) + LN_EPS) * g_ref[...]
        o_ref[:, h * hw:(h + 1) * hw] = (y * (1.0 - lam_init)).astype(o_ref.dtype)


def _diff_attn(q, k, v, lam_q1, lam_k1, lam_q2, lam_k2, subln_g, *, batch, seq, lam_init, tq=256):
    t, d_k = k.shape
    d_v = v.shape[1]
    n_heads = N_ATT_HEADS
    assert d_k == d_v and q.shape[1] == 2 * d_k
    nq = seq // tq
    kern = functools.partial(_attn_kernel, tq=tq, n_heads=n_heads, lam_init=lam_init)
    small = lambda a: pl.BlockSpec(a.shape, lambda b, i: (0, 0))
    n_chains = 2 * n_heads
    return pl.pallas_call(
        kern,
        out_shape=jax.ShapeDtypeStruct((t, d_v), BF16),
        grid=(batch, nq),
        in_specs=[small(lam_q1), small(lam_k1), small(lam_q2), small(lam_k2), small(subln_g),
                  pl.BlockSpec((tq, 2 * d_k), lambda b, i: (b * nq + i, 0)),
                  pl.BlockSpec((seq, d_k), lambda b, i: (b, 0)),
                  pl.BlockSpec((seq, d_v), lambda b, i: (b, 0))],
        out_specs=pl.BlockSpec((tq, d_v), lambda b, i: (b * nq + i, 0)),
        scratch_shapes=[pltpu.VMEM((n_chains, tq, LANES), F32), pltpu.VMEM((n_chains, tq, LANES), F32),
                        pltpu.VMEM((n_chains, tq, d_v // n_heads), F32)],
        compiler_params=pltpu.CompilerParams(dimension_semantics=("arbitrary", "arbitrary"),
                                             vmem_limit_bytes=VMEM_LIMIT_BYTES),
        name="diff_attn",
    )(lam_q1, lam_k1, lam_q2, lam_k2, subln_g, q, k, v)


def _layer_norm(z, g, b):
    mu = jnp.mean(z, axis=1, keepdims=True)
    zc = z - mu
    var = jnp.mean(zc * zc, axis=1, keepdims=True)
    return zc * lax.rsqrt(var + LN_EPS) * g + b


def _outproj_kernel(x_ref, yc_ref, ya_ref, wo_ref, g_ref, b_ref, wq_ref, keys_ref,
                    x1_ref, x1t_ref, sc_ref, *, alpha, d_conv):
    ymix = (jnp.dot(yc_ref[...], wo_ref[:d_conv, :], preferred_element_type=F32)
            + jnp.dot(ya_ref[...], wo_ref[d_conv:, :], preferred_element_type=F32))
    x1 = _layer_norm(alpha * x_ref[...] + ymix, g_ref[...], b_ref[...])
    x1_ref[...] = x1
    x1t_ref[...] = x1.T.astype(BF16)
    n_sub, _, dh = keys_ref.shape
    q = jnp.dot(x1.astype(BF16), wq_ref[...], preferred_element_type=F32).astype(BF16)
    for s in range(n_sub):
        sc_ref[s] = _nt_dot(keys_ref[s], q[:, s * dh:(s + 1) * dh])


def _outproj_query(x2, yconv, yatt, w_out_b, ln_g, ln_b, wq_b, keys_b, *, alpha, tm=512):
    t, d = x2.shape
    d_conv = yconv.shape[1]
    n_sub, nkeys, _ = keys_b.shape
    kern = functools.partial(_outproj_kernel, alpha=alpha, d_conv=d_conv)
    full = lambda a: pl.BlockSpec(a.shape, lambda i: (0,) * a.ndim)
    return pl.pallas_call(
        kern,
        out_shape=(jax.ShapeDtypeStruct((t, d), F32), jax.ShapeDtypeStruct((d, t), BF16),
                   jax.ShapeDtypeStruct((n_sub, nkeys, t), F32)),
        grid=(t // tm,),
        in_specs=[pl.BlockSpec((tm, d), lambda i: (i, 0)),
                  pl.BlockSpec((tm, d_conv), lambda i: (i, 0)),
                  pl.BlockSpec((tm, yatt.shape[1]), lambda i: (i, 0)),
                  full(w_out_b), full(ln_g), full(ln_b), full(wq_b), full(keys_b)],
        out_specs=(pl.BlockSpec((tm, d), lambda i: (i, 0)), pl.BlockSpec((d, tm), lambda i: (0, i)),
                   pl.BlockSpec((n_sub, nkeys, tm), lambda i: (0, 0, i))),
        compiler_params=pltpu.CompilerParams(dimension_semantics=("arbitrary",),
                                             vmem_limit_bytes=VMEM_LIMIT_BYTES),
        name="outproj_query",
    )(x2, yconv, yatt, w_out_b, ln_g, ln_b, wq_b, keys_b)


def _oddeven_merge_sort_pairs(n):
    pairs = []
    p = 1
    while p < n:
        k = p
        while k >= 1:
            for j in range(k % p, n - k, 2 * k):
                for i in range(min(k, n - j - k)):
                    if (i + j) // (2 * p) == (i + j + k) // (2 * p):
                        pairs.append((i + j, i + j + k))
            k //= 2
        p *= 2
    return pairs


def _sort_desc(vals):
    vals = list(vals)
    for i, j in _oddeven_merge_sort_pairs(len(vals)):
        hi = jnp.maximum(vals[i], vals[j])
        lo = jnp.minimum(vals[i], vals[j])
        vals[i], vals[j] = hi, lo
    return vals


def _bitonic_merge_desc(vals):
    vals = list(vals)
    n = len(vals)
    d = n // 2
    while d >= 1:
        for i in range(n):
            if i & d == 0:
                hi = jnp.maximum(vals[i], vals[i + d])
                lo = jnp.minimum(vals[i], vals[i + d])
                vals[i], vals[i + d] = hi, lo
        d //= 2
    return vals


def _top_sorted(s, k):
    n = s.shape[0]
    assert n == k * SUBLANES
    slabs = _sort_desc([s[a * SUBLANES:(a + 1) * SUBLANES, :] for a in range(k)])
    shift = SUBLANES // 2
    while shift >= 1:
        other = [pltpu.roll(x, shift, 0) for x in slabs]
        slabs = _bitonic_merge_desc([jnp.maximum(slabs[r], other[k - 1 - r]) for r in range(k)])
        shift //= 2
    return [x[0:1, :] for x in slabs]


def _next_below(s, kth, k):
    ge = s >= kth
    cnt = jnp.sum(jnp.where(ge, 1.0, 0.0), axis=0, keepdims=True)
    below = jnp.max(jnp.where(ge, NEG_BIG, s), axis=0, keepdims=True)
    return jnp.where(cnt > k, kth, below)


def _count_above(tops, x, *, strict):
    k = len(tops)
    levels = k.bit_length() - 1
    assert k == 1 << levels
    below = (lambda a, p: a < p) if strict else (lambda a, p: a <= p)
    conds = []

    def pivot(level, i, base):
        if i == level:
            return tops[base + (k >> (level + 1)) - 1]
        return jnp.where(conds[i], pivot(level, i + 1, base + (k >> (i + 1))), pivot(level, i + 1, base))

    for level in range(levels):
        conds.append(below(x, pivot(level, 0, 0)))
    count = jnp.zeros_like(x)
    for i, c in enumerate(conds):
        count = count + jnp.where(c, float(k >> (i + 1)), 0.0)
    return jnp.where(below(x, tops[k - 1]), float(k), count)


def _gate_kernel(sc_ref, w1_ref, nsel_ref, w2_ref, rank_ref, *, topk):
    n_sub = sc_ref.shape[0]
    heads = n_sub // 2
    k = topk
    tops = []
    for s_idx in range(n_sub):
        s = sc_ref[s_idx]
        rows = _top_sorted(s, k)
        rows.append(_next_below(s, rows[k - 1], k))
        tops.append([jnp.exp(r - rows[0]) for r in rows])
    a = [jnp.concatenate([tops[2 * h][r] for h in range(heads)], axis=0) for r in range(k + 1)]
    b = [jnp.concatenate([tops[2 * h + 1][r] for h in range(heads)], axis=0) for r in range(k + 1)]
    cands = [a[r] * b[c] for r in range(k + 1) for c in range(k + 1) if (r + 1) * (c + 1) <= k + 1]
    n_pad = 1
    while n_pad < len(cands):
        n_pad *= 2
    pad = jnp.full_like(cands[0], -1.0)
    srt = _sort_desc(cands + [pad] * (n_pad - len(cands)))
    tstar = 0.5 * (srt[k - 1] + srt[k])
    z = jnp.zeros_like(tstar)
    for c in cands:
        z = z + jnp.where(c >= tstar, c, 0.0)
    zinv = 1.0 / z
    for h in range(heads):
        e1 = jnp.exp(sc_ref[2 * h] - jnp.max(sc_ref[2 * h], axis=0, keepdims=True))
        e2 = jnp.exp(sc_ref[2 * h + 1] - jnp.max(sc_ref[2 * h + 1], axis=0, keepdims=True))
        theta = tstar[h:h + 1, :] / e1
        top2 = tops[2 * h + 1][:k]
        rank = _count_above(top2, e2, strict=True)
        nsel = _count_above(top2, theta, strict=False)
        w1_ref[h] = e1
        nsel_ref[h] = nsel
        w2_ref[h] = (e2 * zinv[h:h + 1, :]).astype(w2_ref.dtype)
        rank_ref[h] = rank.astype(rank_ref.dtype)


def _peer_gate(sc, *, topk, tl=256):
    n_sub, nkeys, t = sc.shape
    heads = n_sub // 2
    out_f32 = jax.ShapeDtypeStruct((heads, nkeys, t), F32)
    out_b16 = jax.ShapeDtypeStruct((heads, nkeys, t), BF16)
    spec = pl.BlockSpec((heads, nkeys, tl), lambda i: (0, 0, i))
    return pl.pallas_call(
        functools.partial(_gate_kernel, topk=topk),
        out_shape=(out_f32, out_f32, out_b16, out_b16),
        grid=(t // tl,),
        in_specs=[pl.BlockSpec((n_sub, nkeys, tl), lambda i: (0, 0, i))],
        out_specs=(spec, spec, spec, spec),
        compiler_params=pltpu.CompilerParams(dimension_semantics=("arbitrary",),
                                             vmem_limit_bytes=VMEM_LIMIT_BYTES),
        name="peer_gate",
    )(sc)


def _peer_kernel(x1t_ref, x1_ref, u_ref, vt_ref, w1_ref, nsel_ref, w2_ref, rank_ref, g_ref, b_ref, o_ref,
                 acc_ref, act_ref, ga_ref, w2s_ref, ranks_ref, *, alpha, nkeys):
    e = pl.program_id(1)
    heads = w2_ref.shape[0]
    n_i = w1_ref.shape[1]
    tm = x1t_ref.shape[1]
    n_lt = tm // LANES

    @pl.when(e == 0)
    def _():
        acc_ref[...] = jnp.zeros_like(acc_ref)
        for c in range(n_lt):
            w2s_ref[:, c] = w2_ref[:, :, c * LANES:(c + 1) * LANES]
            ranks_ref[:, c] = rank_ref[:, :, c * LANES:(c + 1) * LANES]

    act = jnp.dot(u_ref[...], x1t_ref[...], preferred_element_type=F32)
    for c in range(n_lt):
        act_ref[c] = act[:, c * LANES:(c + 1) * LANES]

    def chunk(ils, c):
        lanes = pl.ds(pl.multiple_of(c * LANES, LANES), LANES)
        gates = [jnp.zeros((nkeys, LANES), BF16) for _ in ils]
        for h in range(heads):
            rank = ranks_ref[h, c]
            w2 = w2s_ref[h, c]
            for n, il in enumerate(ils):
                nsel = nsel_ref[h, il:il + 1, lanes].astype(BF16)
                w1 = w1_ref[h, il:il + 1, lanes].astype(BF16)
                gates[n] = gates[n] + jnp.where(rank < nsel, w2, jnp.zeros_like(w2)) * w1
        for n, il in enumerate(ils):
            rows = pl.ds(il * nkeys, nkeys)
            a = act_ref[c, rows, :]
            gelu = 0.5 * a * (1.0 + lax.erf(a * math.sqrt(0.5)))
            ga_ref[rows, lanes] = gates[n] * gelu.astype(BF16)

    for il in range(0, n_i, CHUNK_KEYS):
        def lane_tile(c, carry, ils=tuple(range(il, il + CHUNK_KEYS))):
            chunk(ils, c)
            return carry
        lax.fori_loop(0, n_lt, lane_tile, 0)

    acc_ref[...] += jnp.dot(vt_ref[0], ga_ref[...], preferred_element_type=F32)

    @pl.when(e == pl.num_programs(1) - 1)
    def _():
        z = alpha * x1_ref[...] + acc_ref[...].T
        o_ref[...] = _layer_norm(z, g_ref[...], b_ref[...])


def _peer_dense(x1t, x1, u_b, vt_blk, w1, nsel, w2, rank, ln_g, ln_b, *, alpha, tm=1024):
    t, d = x1.shape
    n_eblk, _, be = vt_blk.shape
    heads, nkeys, _ = w2.shape
    n_i = be // nkeys
    n_lt = tm // LANES
    once = pl.Buffered(1)
    kern = functools.partial(_peer_kernel, alpha=alpha, nkeys=nkeys)
    return pl.pallas_call(
        kern,
        out_shape=jax.ShapeDtypeStruct((t, d), F32),
        grid=(t // tm, n_eblk),
        in_specs=[pl.BlockSpec((d, tm), lambda i, e: (0, i), pipeline_mode=once),
                  pl.BlockSpec((tm, d), lambda i, e: (i, 0), pipeline_mode=once),
                  pl.BlockSpec((be, d), lambda i, e: (e, 0)),
                  pl.BlockSpec((1, d, be), lambda i, e: (e, 0, 0)),
                  pl.BlockSpec((heads, n_i, tm), lambda i, e: (0, e, i)),
                  pl.BlockSpec((heads, n_i, tm), lambda i, e: (0, e, i)),
                  pl.BlockSpec((heads, nkeys, tm), lambda i, e: (0, 0, i), pipeline_mode=once),
                  pl.BlockSpec((heads, nkeys, tm), lambda i, e: (0, 0, i), pipeline_mode=once),
                  pl.BlockSpec(ln_g.shape, lambda i, e: (0, 0)),
                  pl.BlockSpec(ln_b.shape, lambda i, e: (0, 0))],
        out_specs=pl.BlockSpec((tm, d), lambda i, e: (i, 0), pipeline_mode=once),
        scratch_shapes=[pltpu.VMEM((d, tm), F32), pltpu.VMEM((n_lt, be, LANES), F32),
                        pltpu.VMEM((be, tm), BF16),
                        pltpu.VMEM((heads, n_lt, nkeys, LANES), BF16),
                        pltpu.VMEM((heads, n_lt, nkeys, LANES), BF16)],
        compiler_params=pltpu.CompilerParams(dimension_semantics=("arbitrary", "arbitrary"),
                                             vmem_limit_bytes=VMEM_LIMIT_BYTES),
        name="peer_dense",
    )(x1t, x1, u_b, vt_blk, w1, nsel, w2, rank, ln_g, ln_b)


def kernel(x, w_in, conv_w, conv_b, lam_q1, lam_k1, lam_q2, lam_k2, subln_g, w_out, ln1_g, ln1_b,
           peer_wq, peer_keys, peer_u, peer_v, ln2_g, ln2_b):
    batch, seq, d = x.shape
    depth = w_in.shape[0]
    d_conv = conv_w.shape[2]
    att_vdim = subln_g.shape[1]
    d_v = N_ATT_HEADS * att_vdim
    d_qk = (w_in.shape[2] - 3 * d_conv - d_v) // 2
    qk_dim = d_qk // (2 * N_ATT_HEADS)
    alpha = (2 * depth) ** 0.25
    heads, _, nkeys, dhalf = peer_keys.shape[1:]
    row = lambda a: a.reshape(1, -1)

    x2 = x.reshape(batch * seq, d)
    for l in range(depth):
        lam_init = 0.8 - 0.6 * math.exp(-0.3 * l)
        qscale = LOG2E / math.sqrt(qk_dim)
        yconv, q, k, v = _inproj_conv(x2, w_in[l].astype(BF16), conv_w[l], row(conv_b[l]), seq=seq,
                                      d_conv=d_conv, d_qk=d_qk, d_v=d_v, qscale=qscale)
        yatt = _diff_attn(q, k, v, row(lam_q1[l]), row(lam_k1[l]), row(lam_q2[l]), row(lam_k2[l]),
                          row(subln_g[l]), batch=batch, seq=seq, lam_init=lam_init)
        x1, x1t, sc = _outproj_query(x2, yconv, yatt, w_out[l].astype(BF16), row(ln1_g[l]), row(ln1_b[l]),
                                     peer_wq[l].astype(BF16),
                                     peer_keys[l].reshape(heads * 2, nkeys, dhalf).astype(BF16), alpha=alpha)
        w1, nsel, w2, rank = _peer_gate(sc, topk=PEER_TOPK)
        vt_blk = peer_v[l].reshape(-1, PEER_EXPERT_BLOCK, d).transpose(0, 2, 1).astype(BF16)
        x2 = _peer_dense(x1t, x1, peer_u[l].astype(BF16), vt_blk, w1, nsel, w2, rank,
                         row(ln2_g[l]), row(ln2_b[l]), alpha=alpha)
    return x2.reshape(batch, seq, d)
```

```python
import functools
import math

import jax
import jax.numpy as jnp
from jax import lax
from jax.experimental import pallas as pl
from jax.experimental.pallas import tpu as pltpu

F32 = jnp.float32
BF16 = jnp.bfloat16

CONV_WIDTH = 3
N_ATT_HEADS = 4
PEER_HEADS = 8
PEER_NKEYS = 128
PEER_TOPK = 16
PEER_EXPERT_BLOCK = 1024
CHUNK_KEYS = 8
CHUNK_ROWS = 128
LN_EPS = 1e-5
LOG2E = 1.4426950408889634
NEG_BIG = -1e30

SUBLANES = 8
LANES = 128
VMEM_LIMIT_BYTES = 56 * 1024 * 1024


def _nt_dot(a, b):
    return lax.dot_general(a, b, (((1,), (1,)), ((), ())), preferred_element_type=F32)


def _inproj_kernel(x_ref, w_ref, cw_ref, cb_ref, yconv_ref, q_ref, k_ref, v_ref, carry_ref,
                   *, tiles_per_seq, d_conv, d_qk, n_heads, qscale):
    i = pl.program_id(0)

    @pl.when(i % tiles_per_seq == 0)
    def _():
        carry_ref[...] = jnp.zeros_like(carry_ref)

    xb = x_ref[...].astype(BF16)

    def proj(lo, width):
        return jnp.dot(xb, w_ref[:, lo:lo + width], preferred_element_type=F32)

    gb = proj(0, d_conv)
    u = proj(d_conv, d_conv) * proj(2 * d_conv, d_conv)
    tm = u.shape[0]
    row = lax.broadcasted_iota(jnp.int32, u.shape, 0)
    c1 = carry_ref[SUBLANES - 1:SUBLANES, :]
    c2 = carry_ref[SUBLANES - 2:SUBLANES - 1, :]
    u1 = jnp.where(row == 0, c1, pltpu.roll(u, 1, 0))
    u2 = jnp.where(row == 0, c2, jnp.where(row == 1, c1, pltpu.roll(u, 2, 0)))
    conv = cb_ref[...] + cw_ref[2:3, :] * u + cw_ref[1:2, :] * u1 + cw_ref[0:1, :] * u2
    carry_ref[...] = u[tm - SUBLANES:, :]
    yconv_ref[...] = (gb * conv).astype(yconv_ref.dtype)

    base = 3 * d_conv
    q = (proj(base, d_qk) * qscale).astype(q_ref.dtype)
    hq = d_qk // n_heads
    lane = lax.broadcasted_iota(jnp.int32, (tm, hq), 1)
    zero = jnp.zeros((tm, hq), q_ref.dtype)
    for h in range(n_heads):
        qh = q[:, h * hq:(h + 1) * hq]
        q_ref[:, (2 * h) * hq:(2 * h + 1) * hq] = jnp.where(lane < hq // 2, qh, zero)
        q_ref[:, (2 * h + 1) * hq:(2 * h + 2) * hq] = jnp.where(lane >= hq // 2, qh, zero)
    k_ref[...] = proj(base + d_qk, d_qk).astype(k_ref.dtype)
    v_ref[...] = proj(base + 2 * d_qk, v_ref.shape[1]).astype(v_ref.dtype)


def _inproj_conv(x2, w_in_b, conv_w, conv_b, *, seq, d_conv, d_qk, d_v, qscale, tm=512):
    t, d = x2.shape
    assert seq % tm == 0
    kern = functools.partial(_inproj_kernel, tiles_per_seq=seq // tm, d_conv=d_conv, d_qk=d_qk,
                             n_heads=N_ATT_HEADS, qscale=qscale)
    return pl.pallas_call(
        kern,
        out_shape=(jax.ShapeDtypeStruct((t, d_conv), BF16), jax.ShapeDtypeStruct((t, 2 * d_qk), BF16),
                   jax.ShapeDtypeStruct((t, d_qk), BF16), jax.ShapeDtypeStruct((t, d_v), BF16)),
        grid=(t // tm,),
        in_specs=[pl.BlockSpec((tm, d), lambda i: (i, 0)),
                  pl.BlockSpec(w_in_b.shape, lambda i: (0, 0)),
                  pl.BlockSpec(conv_w.shape, lambda i: (0, 0)),
                  pl.BlockSpec(conv_b.shape, lambda i: (0, 0))],
        out_specs=(pl.BlockSpec((tm, d_conv), lambda i: (i, 0)), pl.BlockSpec((tm, 2 * d_qk), lambda i: (i, 0)),
                   pl.BlockSpec((tm, d_qk), lambda i: (i, 0)), pl.BlockSpec((tm, d_v), lambda i: (i, 0))),
        scratch_shapes=[pltpu.VMEM((SUBLANES, d_conv), F32)],
        compiler_params=pltpu.CompilerParams(dimension_semantics=("arbitrary",),
                                             vmem_limit_bytes=VMEM_LIMIT_BYTES),
        name="inproj_conv",
    )(x2, w_in_b, conv_w, conv_b)


def _attn_kernel(lq1_ref, lk1_ref, lq2_ref, lk2_ref, g_ref, q_ref, k_ref, v_ref, o_ref,
                 m_ref, l_ref, acc_ref, *, tq, n_heads, lam_init):
    qi = pl.program_id(1)
    hw = k_ref.shape[1] // n_heads
    n_chains = 2 * n_heads

    m_ref[...] = jnp.full(m_ref.shape, NEG_BIG, F32)
    l_ref[...] = jnp.zeros(l_ref.shape, F32)
    acc_ref[...] = jnp.zeros(acc_ref.shape, F32)

    def block(j, diagonal):
        off = pl.multiple_of(j * tq, tq)
        n_slab = tq // LANES
        scores = [_nt_dot(q_ref[:, c * hw:(c + 1) * hw], k_ref[pl.ds(off, tq), (c // 2) * hw:(c // 2 + 1) * hw])
                  for c in range(n_chains)]
        probs, alphas = [], []
        for c in range(n_chains):
            slabs = [scores[c][:, t * LANES:(t + 1) * LANES] for t in range(n_slab)]
            if diagonal:
                r = lax.broadcasted_iota(jnp.int32, (tq, LANES), 0)
                col = lax.broadcasted_iota(jnp.int32, (tq, LANES), 1)
                slabs = [jnp.where(col + t * LANES <= r, sl, NEG_BIG) for t, sl in enumerate(slabs)]
            mx = slabs[0]
            for sl in slabs[1:]:
                mx = jnp.maximum(mx, sl)
            m_old = m_ref[c]
            m_new = jnp.maximum(m_old, jnp.broadcast_to(jnp.max(mx, axis=1, keepdims=True), m_old.shape))
            alpha = jnp.exp2(m_old - m_new)
            ps = [jnp.exp2(sl - m_new) for sl in slabs]
            part = ps[0]
            for p in ps[1:]:
                part = part + p
            m_ref[c] = m_new
            l_ref[c] = alpha * l_ref[c] + part
            probs.append(jnp.concatenate(ps, axis=1).astype(v_ref.dtype))
            alphas.append(alpha)
        for c in range(n_chains):
            vc = v_ref[pl.ds(off, tq), (c // 2) * hw:(c // 2 + 1) * hw]
            acc_ref[c] = alphas[c] * acc_ref[c] + jnp.dot(probs[c], vc, preferred_element_type=F32)

    def body(j, carry):
        block(j, False)
        return carry

    lax.fori_loop(0, qi, body, 0)
    block(qi, True)

    lam = (jnp.exp(jnp.sum(lq1_ref[...] * lk1_ref[...], axis=1, keepdims=True))
           - jnp.exp(jnp.sum(lq2_ref[...] * lk2_ref[...], axis=1, keepdims=True)) + lam_init)
    for h in range(n_heads):
        o1 = acc_ref[2 * h] / jnp.sum(l_ref[2 * h], axis=1, keepdims=True)
        o2 = acc_ref[2 * h + 1] / jnp.sum(l_ref[2 * h + 1], axis=1, keepdims=True)
        y = o1 - lam * o2
        y = y * lax.rsqrt(jnp.mean(y * y, axis=1, keepdims=True) + LN_EPS) * g_ref[...]
        o_ref[:, h * hw:(h + 1) * hw] = (y * (1.0 - lam_init)).astype(o_ref.dtype)


def _diff_attn(q, k, v, lam_q1, lam_k1, lam_q2, lam_k2, subln_g, *, batch, seq, lam_init, tq=256):
    t, d_k = k.shape
    d_v = v.shape[1]
    n_heads = N_ATT_HEADS
    assert d_k == d_v and q.shape[1] == 2 * d_k
    nq = seq // tq
    kern = functools.partial(_attn_kernel, tq=tq, n_heads=n_heads, lam_init=lam_init)
    small = lambda a: pl.BlockSpec(a.shape, lambda b, i: (0, 0))
    n_chains = 2 * n_heads
    return pl.pallas_call(
        kern,
        out_shape=jax.ShapeDtypeStruct((t, d_v), BF16),
        grid=(batch, nq),
        in_specs=[small(lam_q1), small(lam_k1), small(lam_q2), small(lam_k2), small(subln_g),
                  pl.BlockSpec((tq, 2 * d_k), lambda b, i: (b * nq + i, 0)),
                  pl.BlockSpec((seq, d_k), lambda b, i: (b, 0)),
                  pl.BlockSpec((seq, d_v), lambda b, i: (b, 0))],
        out_specs=pl.BlockSpec((tq, d_v), lambda b, i: (b * nq + i, 0)),
        scratch_shapes=[pltpu.VMEM((n_chains, tq, LANES), F32), pltpu.VMEM((n_chains, tq, LANES), F32),
                        pltpu.VMEM((n_chains, tq, d_v // n_heads), F32)],
        compiler_params=pltpu.CompilerParams(dimension_semantics=("arbitrary", "arbitrary"),
                                             vmem_limit_bytes=VMEM_LIMIT_BYTES),
        name="diff_attn",
    )(lam_q1, lam_k1, lam_q2, lam_k2, subln_g, q, k, v)


def _layer_norm(z, g, b):
    mu = jnp.mean(z, axis=1, keepdims=True)
    zc = z - mu
    var = jnp.mean(zc * zc, axis=1, keepdims=True)
    return zc * lax.rsqrt(var + LN_EPS) * g + b


def _outproj_kernel(x_ref, yc_ref, ya_ref, wo_ref, g_ref, b_ref, wq_ref, keys_ref,
                    x1_ref, x1t_ref, sc_ref, *, alpha, d_conv):
    ymix = (jnp.dot(yc_ref[...], wo_ref[:d_conv, :], preferred_element_type=F32)
            + jnp.dot(ya_ref[...], wo_ref[d_conv:, :], preferred_element_type=F32))
    x1 = _layer_norm(alpha * x_ref[...] + ymix, g_ref[...], b_ref[...])
    x1_ref[...] = x1
    x1t_ref[...] = x1.T.astype(BF16)
    n_sub, _, dh = keys_ref.shape
    q = jnp.dot(x1.astype(BF16), wq_ref[...], preferred_element_type=F32).astype(BF16)
    for s in range(n_sub):
        sc_ref[s] = _nt_dot(keys_ref[s], q[:, s * dh:(s + 1) * dh])


def _outproj_query(x2, yconv, yatt, w_out_b, ln_g, ln_b, wq_b, keys_b, *, alpha, tm=512):
    t, d = x2.shape
    d_conv = yconv.shape[1]
    n_sub, nkeys, _ = keys_b.shape
    kern = functools.partial(_outproj_kernel, alpha=alpha, d_conv=d_conv)
    full = lambda a: pl.BlockSpec(a.shape, lambda i: (0,) * a.ndim)
    return pl.pallas_call(
        kern,
        out_shape=(jax.ShapeDtypeStruct((t, d), F32), jax.ShapeDtypeStruct((d, t), BF16),
                   jax.ShapeDtypeStruct((n_sub, nkeys, t), F32)),
        grid=(t // tm,),
        in_specs=[pl.BlockSpec((tm, d), lambda i: (i, 0)),
                  pl.BlockSpec((tm, d_conv), lambda i: (i, 0)),
                  pl.BlockSpec((tm, yatt.shape[1]), lambda i: (i, 0)),
                  full(w_out_b), full(ln_g), full(ln_b), full(wq_b), full(keys_b)],
        out_specs=(pl.BlockSpec((tm, d), lambda i: (i, 0)), pl.BlockSpec((d, tm), lambda i: (0, i)),
                   pl.BlockSpec((n_sub, nkeys, tm), lambda i: (0, 0, i))),
        compiler_params=pltpu.CompilerParams(dimension_semantics=("arbitrary",),
                                             vmem_limit_bytes=VMEM_LIMIT_BYTES),
        name="outproj_query",
    )(x2, yconv, yatt, w_out_b, ln_g, ln_b, wq_b, keys_b)


def _oddeven_merge_sort_pairs(n):
    pairs = []
    p = 1
    while p < n:
        k = p
        while k >= 1:
            for j in range(k % p, n - k, 2 * k):
                for i in range(min(k, n - j - k)):
                    if (i + j) // (2 * p) == (i + j + k) // (2 * p):
                        pairs.append((i + j, i + j + k))
            k //= 2
        p *= 2
    return pairs


def _sort_desc(vals):
    vals = list(vals)
    for i, j in _oddeven_merge_sort_pairs(len(vals)):
        hi = jnp.maximum(vals[i], vals[j])
        lo = jnp.minimum(vals[i], vals[j])
        vals[i], vals[j] = hi, lo
    return vals


def _bitonic_merge_desc(vals):
    vals = list(vals)
    n = len(vals)
    d = n // 2
    while d >= 1:
        for i in range(n):
            if i & d == 0:
                hi = jnp.maximum(vals[i], vals[i + d])
                lo = jnp.minimum(vals[i], vals[i + d])
                vals[i], vals[i + d] = hi, lo
        d //= 2
    return vals


def _top_sorted(s, k):
    n = s.shape[0]
    assert n == k * SUBLANES
    slabs = _sort_desc([s[a * SUBLANES:(a + 1) * SUBLANES, :] for a in range(k)])
    shift = SUBLANES // 2
    while shift >= 1:
        other = [pltpu.roll(x, shift, 0) for x in slabs]
        slabs = _bitonic_merge_desc([jnp.maximum(slabs[r], other[k - 1 - r]) for r in range(k)])
        shift //= 2
    return [x[0:1, :] for x in slabs]


def _next_below(s, kth, k):
    ge = s >= kth
    cnt = jnp.sum(jnp.where(ge, 1.0, 0.0), axis=0, keepdims=True)
    below = jnp.max(jnp.where(ge, NEG_BIG, s), axis=0, keepdims=True)
    return jnp.where(cnt > k, kth, below)


def _count_above(tops, x, *, strict):
    k = len(tops)
    levels = k.bit_length() - 1
    assert k == 1 << levels
    below = (lambda a, p: a < p) if strict else (lambda a, p: a <= p)
    conds = []

    def pivot(level, i, base):
        if i == level:
            return tops[base + (k >> (level + 1)) - 1]
        return jnp.where(conds[i], pivot(level, i + 1, base + (k >> (i + 1))), pivot(level, i + 1, base))

    for level in range(levels):
        conds.append(below(x, pivot(level, 0, 0)))
    count = jnp.zeros_like(x)
    for i, c in enumerate(conds):
        count = count + jnp.where(c, float(k >> (i + 1)), 0.0)
    return jnp.where(below(x, tops[k - 1]), float(k), count)


def _gate_kernel(sc_ref, w1_ref, nsel_ref, w2_ref, rank_ref, *, topk):
    n_sub = sc_ref.shape[0]
    heads = n_sub // 2
    k = topk
    tops = []
    for s_idx in range(n_sub):
        s = sc_ref[s_idx]
        rows = _top_sorted(s, k)
        rows.append(_next_below(s, rows[k - 1], k))
        tops.append([jnp.exp(r - rows[0]) for r in rows])
    a = [jnp.concatenate([tops[2 * h][r] for h in range(heads)], axis=0) for r in range(k + 1)]
    b = [jnp.concatenate([tops[2 * h + 1][r] for h in range(heads)], axis=0) for r in range(k + 1)]
    cands = [a[r] * b[c] for r in range(k + 1) for c in range(k + 1) if (r + 1) * (c + 1) <= k + 1]
    n_pad = 1
    while n_pad < len(cands):
        n_pad *= 2
    pad = jnp.full_like(cands[0], -1.0)
    srt = _sort_desc(cands + [pad] * (n_pad - len(cands)))
    tstar = 0.5 * (srt[k - 1] + srt[k])
    z = jnp.zeros_like(tstar)
    for c in cands:
        z = z + jnp.where(c >= tstar, c, 0.0)
    zinv = 1.0 / z
    for h in range(heads):
        e1 = jnp.exp(sc_ref[2 * h] - jnp.max(sc_ref[2 * h], axis=0, keepdims=True))
        e2 = jnp.exp(sc_ref[2 * h + 1] - jnp.max(sc_ref[2 * h + 1], axis=0, keepdims=True))
        theta = tstar[h:h + 1, :] / e1
        top2 = tops[2 * h + 1][:k]
        rank = _count_above(top2, e2, strict=True)
        nsel = _count_above(top2, theta, strict=False)
        w1_ref[h] = e1
        nsel_ref[h] = nsel
        w2_ref[h] = (e2 * (0.5 * zinv[h:h + 1, :])).astype(w2_ref.dtype)
        rank_ref[h] = rank.astype(rank_ref.dtype)


def _peer_gate(sc, *, topk, tl=256):
    n_sub, nkeys, t = sc.shape
    heads = n_sub // 2
    out_f32 = jax.ShapeDtypeStruct((heads, nkeys, t), F32)
    out_b16 = jax.ShapeDtypeStruct((heads, nkeys, t), BF16)
    spec = pl.BlockSpec((heads, nkeys, tl), lambda i: (0, 0, i))
    return pl.pallas_call(
        functools.partial(_gate_kernel, topk=topk),
        out_shape=(out_f32, out_f32, out_b16, out_b16),
        grid=(t // tl,),
        in_specs=[pl.BlockSpec((n_sub, nkeys, tl), lambda i: (0, 0, i))],
        out_specs=(spec, spec, spec, spec),
        compiler_params=pltpu.CompilerParams(dimension_semantics=("arbitrary",),
                                             vmem_limit_bytes=VMEM_LIMIT_BYTES),
        name="peer_gate",
    )(sc)


def _peer_kernel(x1t_ref, x1_ref, u_ref, vt_ref, w1_ref, nsel_ref, w2_ref, rank_ref, g_ref, b_ref, o_ref,
                 acc_ref, act_ref, ga_ref, w2s_ref, ranks_ref, *, alpha, nkeys):
    e = pl.program_id(1)
    heads = w2_ref.shape[0]
    n_i = w1_ref.shape[1]
    tm = x1t_ref.shape[1]
    n_lt = tm // LANES

    @pl.when(e == 0)
    def _():
        acc_ref[...] = jnp.zeros_like(acc_ref)
        for c in range(n_lt):
            w2s_ref[:, c] = w2_ref[:, :, c * LANES:(c + 1) * LANES]
            ranks_ref[:, c] = rank_ref[:, :, c * LANES:(c + 1) * LANES]

    act = jnp.dot(u_ref[...], x1t_ref[...], preferred_element_type=F32)
    for c in range(n_lt):
        act_ref[c] = act[:, c * LANES:(c + 1) * LANES]

    def chunk(ils, j0, c):
        lanes = pl.ds(pl.multiple_of(c * LANES, LANES), LANES)
        keys = pl.ds(j0, CHUNK_ROWS)
        gates = [jnp.zeros((CHUNK_ROWS, LANES), BF16) for _ in ils]
        for h in range(heads):
            rank = ranks_ref[h, c, keys, :]
            w2 = w2s_ref[h, c, keys, :]
            for n, il in enumerate(ils):
                nsel = nsel_ref[h, il:il + 1, lanes].astype(BF16)
                w1 = w1_ref[h, il:il + 1, lanes].astype(BF16)
                gates[n] = gates[n] + jnp.where(rank < nsel, w2, jnp.zeros_like(w2)) * w1
        for n, il in enumerate(ils):
            rows = pl.ds(il * nkeys + j0, CHUNK_ROWS)
            a = act_ref[c, rows, :]
            gelu2 = a + a * lax.erf(a * math.sqrt(0.5))
            ga_ref[rows, lanes] = gates[n] * gelu2.astype(BF16)

    for il in range(0, n_i, CHUNK_KEYS):
        for j0 in range(0, nkeys, CHUNK_ROWS):
            def lane_tile(c, carry, ils=tuple(range(il, il + CHUNK_KEYS)), j0=j0):
                chunk(ils, j0, c)
                return carry
            lax.fori_loop(0, n_lt, lane_tile, 0)

    acc_ref[...] += jnp.dot(vt_ref[0], ga_ref[...], preferred_element_type=F32)

    @pl.when(e == pl.num_programs(1) - 1)
    def _():
        z = alpha * x1_ref[...] + acc_ref[...].T
        o_ref[...] = _layer_norm(z, g_ref[...], b_ref[...])


def _peer_dense(x1t, x1, u_b, vt_blk, w1, nsel, w2, rank, ln_g, ln_b, *, alpha, tm=1024):
    t, d = x1.shape
    n_eblk, _, be = vt_blk.shape
    heads, nkeys, _ = w2.shape
    n_i = be // nkeys
    n_lt = tm // LANES
    once = pl.Buffered(1)
    kern = functools.partial(_peer_kernel, alpha=alpha, nkeys=nkeys)
    return pl.pallas_call(
        kern,
        out_shape=jax.ShapeDtypeStruct((t, d), F32),
        grid=(t // tm, n_eblk),
        in_specs=[pl.BlockSpec((d, tm), lambda i, e: (0, i), pipeline_mode=once),
                  pl.BlockSpec((tm, d), lambda i, e: (i, 0), pipeline_mode=once),
                  pl.BlockSpec((be, d), lambda i, e: (e, 0)),
                  pl.BlockSpec((1, d, be), lambda i, e: (e, 0, 0)),
                  pl.BlockSpec((heads, n_i, tm), lambda i, e: (0, e, i)),
                  pl.BlockSpec((heads, n_i, tm), lambda i, e: (0, e, i)),
                  pl.BlockSpec((heads, nkeys, tm), lambda i, e: (0, 0, i), pipeline_mode=once),
                  pl.BlockSpec((heads, nkeys, tm), lambda i, e: (0, 0, i), pipeline_mode=once),
                  pl.BlockSpec(ln_g.shape, lambda i, e: (0, 0)),
                  pl.BlockSpec(ln_b.shape, lambda i, e: (0, 0))],
        out_specs=pl.BlockSpec((tm, d), lambda i, e: (i, 0), pipeline_mode=once),
        scratch_shapes=[pltpu.VMEM((d, tm), F32), pltpu.VMEM((n_lt, be, LANES), F32),
                        pltpu.VMEM((be, tm), BF16),
                        pltpu.VMEM((heads, n_lt, nkeys, LANES), BF16),
                        pltpu.VMEM((heads, n_lt, nkeys, LANES), BF16)],
        compiler_params=pltpu.CompilerParams(dimension_semantics=("arbitrary", "arbitrary"),
                                             vmem_limit_bytes=VMEM_LIMIT_BYTES),
        name="peer_dense",
    )(x1t, x1, u_b, vt_blk, w1, nsel, w2, rank, ln_g, ln_b)


def kernel(x, w_in, conv_w, conv_b, lam_q1, lam_k1, lam_q2, lam_k2, subln_g, w_out, ln1_g, ln1_b,
           peer_wq, peer_keys, peer_u, peer_v, ln2_g, ln2_b):
    batch, seq, d = x.shape
    depth = w_in.shape[0]
    d_conv = conv_w.shape[2]
    att_vdim = subln_g.shape[1]
    d_v = N_ATT_HEADS * att_vdim
    d_qk = (w_in.shape[2] - 3 * d_conv - d_v) // 2
    qk_dim = d_qk // (2 * N_ATT_HEADS)
    alpha = (2 * depth) ** 0.25
    heads, _, nkeys, dhalf = peer_keys.shape[1:]
    row = lambda a: a.reshape(1, -1)

    x2 = x.reshape(batch * seq, d)
    for l in range(depth):
        lam_init = 0.8 - 0.6 * math.exp(-0.3 * l)
        qscale = LOG2E / math.sqrt(qk_dim)
        yconv, q, k, v = _inproj_conv(x2, w_in[l].astype(BF16), conv_w[l], row(conv_b[l]), seq=seq,
                                      d_conv=d_conv, d_qk=d_qk, d_v=d_v, qscale=qscale)
        yatt = _diff_attn(q, k, v, row(lam_q1[l]), row(lam_k1[l]), row(lam_q2[l]), row(lam_k2[l]),
                          row(subln_g[l]), batch=batch, seq=seq, lam_init=lam_init)
        x1, x1t, sc = _outproj_query(x2, yconv, yatt, w_out[l].astype(BF16), row(ln1_g[l]), row(ln1_b[l]),
                                     peer_wq[l].astype(BF16),
                                     peer_keys[l].reshape(heads * 2, nkeys, dhalf).astype(BF16), alpha=alpha)
        w1, nsel, w2, rank = _peer_gate(sc, topk=PEER_TOPK)
        vt_blk = peer_v[l].reshape(-1, PEER_EXPERT_BLOCK, d).transpose(0, 2, 1).astype(BF16)
        x2 = _peer_dense(x1t, x1, peer_u[l].astype(BF16), vt_blk, w1, nsel, w2, rank,
                         row(ln2_g[l]), row(ln2_b[l]), alpha=alpha)
    return x2.reshape(batch, seq, d)
```

```python
import functools
import math

import jax
import jax.numpy as jnp
from jax import lax
from jax.experimental import pallas as pl
from jax.experimental.pallas import tpu as pltpu

F32 = jnp.float32
BF16 = jnp.bfloat16

CONV_WIDTH = 3
N_ATT_HEADS = 4
PEER_HEADS = 8
PEER_NKEYS = 128
PEER_TOPK = 16
PEER_EXPERT_BLOCK = 1024
CHUNK_KEYS = 8
CHUNK_ROWS = 128
LN_EPS = 1e-5
LOG2E = 1.4426950408889634
NEG_BIG = -1e30

SUBLANES = 8
LANES = 128
VMEM_LIMIT_BYTES = 56 * 1024 * 1024


def _nt_dot(a, b):
    return lax.dot_general(a, b, (((1,), (1,)), ((), ())), preferred_element_type=F32)


def _inproj_kernel(x_ref, w_ref, cw_ref, cb_ref, yconv_ref, q_ref, k_ref, v_ref, carry_ref,
                   *, tiles_per_seq, d_conv, d_qk, n_heads, qscale):
    i = pl.program_id(0)

    @pl.when(i % tiles_per_seq == 0)
    def _():
        carry_ref[...] = jnp.zeros_like(carry_ref)

    xb = x_ref[...].astype(BF16)

    def proj(lo, width):
        return jnp.dot(xb, w_ref[:, lo:lo + width], preferred_element_type=F32)

    gb = proj(0, d_conv)
    u = proj(d_conv, d_conv) * proj(2 * d_conv, d_conv)
    tm = u.shape[0]
    row = lax.broadcasted_iota(jnp.int32, u.shape, 0)
    c1 = carry_ref[SUBLANES - 1:SUBLANES, :]
    c2 = carry_ref[SUBLANES - 2:SUBLANES - 1, :]
    u1 = jnp.where(row == 0, c1, pltpu.roll(u, 1, 0))
    u2 = jnp.where(row == 0, c2, jnp.where(row == 1, c1, pltpu.roll(u, 2, 0)))
    conv = cb_ref[...] + cw_ref[2:3, :] * u + cw_ref[1:2, :] * u1 + cw_ref[0:1, :] * u2
    carry_ref[...] = u[tm - SUBLANES:, :]
    yconv_ref[...] = (gb * conv).astype(yconv_ref.dtype)

    base = 3 * d_conv
    q = (proj(base, d_qk) * qscale).astype(q_ref.dtype)
    hq = d_qk // n_heads
    lane = lax.broadcasted_iota(jnp.int32, (tm, hq), 1)
    zero = jnp.zeros((tm, hq), q_ref.dtype)
    for h in range(n_heads):
        qh = q[:, h * hq:(h + 1) * hq]
        q_ref[:, (2 * h) * hq:(2 * h + 1) * hq] = jnp.where(lane < hq // 2, qh, zero)
        q_ref[:, (2 * h + 1) * hq:(2 * h + 2) * hq] = jnp.where(lane >= hq // 2, qh, zero)
    k_ref[...] = proj(base + d_qk, d_qk).astype(k_ref.dtype)
    v_ref[...] = proj(base + 2 * d_qk, v_ref.shape[1]).astype(v_ref.dtype)


def _inproj_conv(x2, w_in_b, conv_w, conv_b, *, seq, d_conv, d_qk, d_v, qscale, tm=512):
    t, d = x2.shape
    assert seq % tm == 0
    kern = functools.partial(_inproj_kernel, tiles_per_seq=seq // tm, d_conv=d_conv, d_qk=d_qk,
                             n_heads=N_ATT_HEADS, qscale=qscale)
    return pl.pallas_call(
        kern,
        out_shape=(jax.ShapeDtypeStruct((t, d_conv), BF16), jax.ShapeDtypeStruct((t, 2 * d_qk), BF16),
                   jax.ShapeDtypeStruct((t, d_qk), BF16), jax.ShapeDtypeStruct((t, d_v), BF16)),
        grid=(t // tm,),
        in_specs=[pl.BlockSpec((tm, d), lambda i: (i, 0)),
                  pl.BlockSpec(w_in_b.shape, lambda i: (0, 0)),
                  pl.BlockSpec(conv_w.shape, lambda i: (0, 0)),
                  pl.BlockSpec(conv_b.shape, lambda i: (0, 0))],
        out_specs=(pl.BlockSpec((tm, d_conv), lambda i: (i, 0)), pl.BlockSpec((tm, 2 * d_qk), lambda i: (i, 0)),
                   pl.BlockSpec((tm, d_qk), lambda i: (i, 0)), pl.BlockSpec((tm, d_v), lambda i: (i, 0))),
        scratch_shapes=[pltpu.VMEM((SUBLANES, d_conv), F32)],
        compiler_params=pltpu.CompilerParams(dimension_semantics=("arbitrary",),
                                             vmem_limit_bytes=VMEM_LIMIT_BYTES),
        name="inproj_conv",
    )(x2, w_in_b, conv_w, conv_b)


def _attn_kernel(lq1_ref, lk1_ref, lq2_ref, lk2_ref, g_ref, q_ref, k_ref, v_ref, o_ref,
                 m_ref, l_ref, acc_ref, *, tq, n_heads, lam_init):
    qi = pl.program_id(1)
    hw = k_ref.shape[1] // n_heads
    n_chains = 2 * n_heads

    m_ref[...] = jnp.full(m_ref.shape, NEG_BIG, F32)
    l_ref[...] = jnp.zeros(l_ref.shape, F32)
    acc_ref[...] = jnp.zeros(acc_ref.shape, F32)

    def block(j, diagonal):
        off = pl.multiple_of(j * tq, tq)
        n_slab = tq // LANES
        scores = [_nt_dot(q_ref[:, c * hw:(c + 1) * hw], k_ref[pl.ds(off, tq), (c // 2) * hw:(c // 2 + 1) * hw])
                  for c in range(n_chains)]
        probs, alphas = [], []
        for c in range(n_chains):
            slabs = [scores[c][:, t * LANES:(t + 1) * LANES] for t in range(n_slab)]
            if diagonal:
                r = lax.broadcasted_iota(jnp.int32, (tq, LANES), 0)
                col = lax.broadcasted_iota(jnp.int32, (tq, LANES), 1)
                slabs = [jnp.where(col + t * LANES <= r, sl, NEG_BIG) for t, sl in enumerate(slabs)]
            mx = slabs[0]
            for sl in slabs[1:]:
                mx = jnp.maximum(mx, sl)
            m_old = m_ref[c]
            m_new = jnp.maximum(m_old, jnp.broadcast_to(jnp.max(mx, axis=1, keepdims=True), m_old.shape))
            alpha = jnp.exp2(m_old - m_new)
            ps = [jnp.exp2(sl - m_new) for sl in slabs]
            part = ps[0]
            for p in ps[1:]:
                part = part + p
            m_ref[c] = m_new
            l_ref[c] = alpha * l_ref[c] + part
            probs.append(jnp.concatenate(ps, axis=1).astype(v_ref.dtype))
            alphas.append(alpha)
        for c in range(n_chains):
            vc = v_ref[pl.ds(off, tq), (c // 2) * hw:(c // 2 + 1) * hw]
            acc_ref[c] = alphas[c] * acc_ref[c] + jnp.dot(probs[c], vc, preferred_element_type=F32)

    def body(j, carry):
        block(j, False)
        return carry

    lax.fori_loop(0, qi, body, 0)
    block(qi, True)

    lam = (jnp.exp(jnp.sum(lq1_ref[...] * lk1_ref[...], axis=1, keepdims=True))
           - jnp.exp(jnp.sum(lq2_ref[...] * lk2_ref[...], axis=1, keepdims=True)) + lam_init)
    for h in range(n_heads):
        o1 = acc_ref[2 * h] / jnp.sum(l_ref[2 * h], axis=1, keepdims=True)
        o2 = acc_ref[2 * h + 1] / jnp.sum(l_ref[2 * h + 1], axis=1, keepdims=True)
        y = o1 - lam * o2
        y = y * lax.rsqrt(jnp.mean(y * y, axis=1, keepdims=True) + LN_EPS) * g_ref[...]
        o_ref[:, h * hw:(h + 1) * hw] = (y * (1.0 - lam_init)).astype(o_ref.dtype)


def _diff_attn(q, k, v, lam_q1, lam_k1, lam_q2, lam_k2, subln_g, *, batch, seq, lam_init, tq=512):
    t, d_k = k.shape
    d_v = v.shape[1]
    n_heads = N_ATT_HEADS
    assert d_k == d_v and q.shape[1] == 2 * d_k
    nq = seq // tq
    kern = functools.partial(_attn_kernel, tq=tq, n_heads=n_heads, lam_init=lam_init)
    small = lambda a: pl.BlockSpec(a.shape, lambda b, i: (0, 0))
    n_chains = 2 * n_heads
    return pl.pallas_call(
        kern,
        out_shape=jax.ShapeDtypeStruct((t, d_v), BF16),
        grid=(batch, nq),
        in_specs=[small(lam_q1), small(lam_k1), small(lam_q2), small(lam_k2), small(subln_g),
                  pl.BlockSpec((tq, 2 * d_k), lambda b, i: (b * nq + i, 0)),
                  pl.BlockSpec((seq, d_k), lambda b, i: (b, 0)),
                  pl.BlockSpec((seq, d_v), lambda b, i: (b, 0))],
        out_specs=pl.BlockSpec((tq, d_v), lambda b, i: (b * nq + i, 0)),
        scratch_shapes=[pltpu.VMEM((n_chains, tq, LANES), F32), pltpu.VMEM((n_chains, tq, LANES), F32),
                        pltpu.VMEM((n_chains, tq, d_v // n_heads), F32)],
        compiler_params=pltpu.CompilerParams(dimension_semantics=("arbitrary", "arbitrary"),
                                             vmem_limit_bytes=VMEM_LIMIT_BYTES),
        name="diff_attn",
    )(lam_q1, lam_k1, lam_q2, lam_k2, subln_g, q, k, v)


def _layer_norm(z, g, b):
    mu = jnp.mean(z, axis=1, keepdims=True)
    zc = z - mu
    var = jnp.mean(zc * zc, axis=1, keepdims=True)
    return zc * lax.rsqrt(var + LN_EPS) * g + b


def _outproj_kernel(x_ref, yc_ref, ya_ref, wo_ref, g_ref, b_ref, wq_ref, keys_ref,
                    x1_ref, x1t_ref, sc_ref, *, alpha, d_conv):
    ymix = (jnp.dot(yc_ref[...], wo_ref[:d_conv, :], preferred_element_type=F32)
            + jnp.dot(ya_ref[...], wo_ref[d_conv:, :], preferred_element_type=F32))
    x1 = _layer_norm(alpha * x_ref[...] + ymix, g_ref[...], b_ref[...])
    x1_ref[...] = x1
    x1t_ref[...] = x1.T.astype(BF16)
    n_sub, _, dh = keys_ref.shape
    q = jnp.dot(x1.astype(BF16), wq_ref[...], preferred_element_type=F32).astype(BF16)
    for s in range(n_sub):
        sc_ref[s] = _nt_dot(keys_ref[s], q[:, s * dh:(s + 1) * dh])


def _outproj_query(x2, yconv, yatt, w_out_b, ln_g, ln_b, wq_b, keys_b, *, alpha, tm=512):
    t, d = x2.shape
    d_conv = yconv.shape[1]
    n_sub, nkeys, _ = keys_b.shape
    kern = functools.partial(_outproj_kernel, alpha=alpha, d_conv=d_conv)
    full = lambda a: pl.BlockSpec(a.shape, lambda i: (0,) * a.ndim)
    return pl.pallas_call(
        kern,
        out_shape=(jax.ShapeDtypeStruct((t, d), F32), jax.ShapeDtypeStruct((d, t), BF16),
                   jax.ShapeDtypeStruct((n_sub, nkeys, t), F32)),
        grid=(t // tm,),
        in_specs=[pl.BlockSpec((tm, d), lambda i: (i, 0)),
                  pl.BlockSpec((tm, d_conv), lambda i: (i, 0)),
                  pl.BlockSpec((tm, yatt.shape[1]), lambda i: (i, 0)),
                  full(w_out_b), full(ln_g), full(ln_b), full(wq_b), full(keys_b)],
        out_specs=(pl.BlockSpec((tm, d), lambda i: (i, 0)), pl.BlockSpec((d, tm), lambda i: (0, i)),
                   pl.BlockSpec((n_sub, nkeys, tm), lambda i: (0, 0, i))),
        compiler_params=pltpu.CompilerParams(dimension_semantics=("arbitrary",),
                                             vmem_limit_bytes=VMEM_LIMIT_BYTES),
        name="outproj_query",
    )(x2, yconv, yatt, w_out_b, ln_g, ln_b, wq_b, keys_b)


def _oddeven_merge_sort_pairs(n):
    pairs = []
    p = 1
    while p < n:
        k = p
        while k >= 1:
            for j in range(k % p, n - k, 2 * k):
                for i in range(min(k, n - j - k)):
                    if (i + j) // (2 * p) == (i + j + k) // (2 * p):
                        pairs.append((i + j, i + j + k))
            k //= 2
        p *= 2
    return pairs


def _sort_desc(vals):
    vals = list(vals)
    for i, j in _oddeven_merge_sort_pairs(len(vals)):
        hi = jnp.maximum(vals[i], vals[j])
        lo = jnp.minimum(vals[i], vals[j])
        vals[i], vals[j] = hi, lo
    return vals


def _bitonic_merge_desc(vals):
    vals = list(vals)
    n = len(vals)
    d = n // 2
    while d >= 1:
        for i in range(n):
            if i & d == 0:
                hi = jnp.maximum(vals[i], vals[i + d])
                lo = jnp.minimum(vals[i], vals[i + d])
                vals[i], vals[i + d] = hi, lo
        d //= 2
    return vals


def _top_sorted(s, k):
    n = s.shape[0]
    assert n == k * SUBLANES
    slabs = _sort_desc([s[a * SUBLANES:(a + 1) * SUBLANES, :] for a in range(k)])
    shift = SUBLANES // 2
    while shift >= 1:
        other = [pltpu.roll(x, shift, 0) for x in slabs]
        slabs = _bitonic_merge_desc([jnp.maximum(slabs[r], other[k - 1 - r]) for r in range(k)])
        shift //= 2
    return [x[0:1, :] for x in slabs]


def _next_below(s, kth, k):
    ge = s >= kth
    cnt = jnp.sum(jnp.where(ge, 1.0, 0.0), axis=0, keepdims=True)
    below = jnp.max(jnp.where(ge, NEG_BIG, s), axis=0, keepdims=True)
    return jnp.where(cnt > k, kth, below)


def _count_above(tops, x, *, strict):
    k = len(tops)
    levels = k.bit_length() - 1
    assert k == 1 << levels
    below = (lambda a, p: a < p) if strict else (lambda a, p: a <= p)
    conds = []

    def pivot(level, i, base):
        if i == level:
            return tops[base + (k >> (level + 1)) - 1]
        return jnp.where(conds[i], pivot(level, i + 1, base + (k >> (i + 1))), pivot(level, i + 1, base))

    for level in range(levels):
        conds.append(below(x, pivot(level, 0, 0)))
    count = jnp.zeros_like(x)
    for i, c in enumerate(conds):
        count = count + jnp.where(c, float(k >> (i + 1)), 0.0)
    return jnp.where(below(x, tops[k - 1]), float(k), count)


def _gate_kernel(sc_ref, w1_ref, nsel_ref, w2_ref, rank_ref, *, topk):
    n_sub = sc_ref.shape[0]
    heads = n_sub // 2
    k = topk
    tops = []
    for s_idx in range(n_sub):
        s = sc_ref[s_idx]
        rows = _top_sorted(s, k)
        rows.append(_next_below(s, rows[k - 1], k))
        tops.append([jnp.exp(r - rows[0]) for r in rows])
    a = [jnp.concatenate([tops[2 * h][r] for h in range(heads)], axis=0) for r in range(k + 1)]
    b = [jnp.concatenate([tops[2 * h + 1][r] for h in range(heads)], axis=0) for r in range(k + 1)]
    cands = [a[r] * b[c] for r in range(k + 1) for c in range(k + 1) if (r + 1) * (c + 1) <= k + 1]
    n_pad = 1
    while n_pad < len(cands):
        n_pad *= 2
    pad = jnp.full_like(cands[0], -1.0)
    srt = _sort_desc(cands + [pad] * (n_pad - len(cands)))
    tstar = 0.5 * (srt[k - 1] + srt[k])
    z = jnp.zeros_like(tstar)
    for c in cands:
        z = z + jnp.where(c >= tstar, c, 0.0)
    zinv = 1.0 / z
    for h in range(heads):
        e1 = jnp.exp(sc_ref[2 * h] - jnp.max(sc_ref[2 * h], axis=0, keepdims=True))
        e2 = jnp.exp(sc_ref[2 * h + 1] - jnp.max(sc_ref[2 * h + 1], axis=0, keepdims=True))
        theta = tstar[h:h + 1, :] / e1
        top2 = tops[2 * h + 1][:k]
        rank = _count_above(top2, e2, strict=True)
        nsel = _count_above(top2, theta, strict=False)
        w1_ref[h] = e1
        nsel_ref[h] = nsel
        w2_ref[h] = (e2 * (0.5 * zinv[h:h + 1, :])).astype(w2_ref.dtype)
        rank_ref[h] = rank.astype(rank_ref.dtype)


def _peer_gate(sc, *, topk, tl=256):
    n_sub, nkeys, t = sc.shape
    heads = n_sub // 2
    out_f32 = jax.ShapeDtypeStruct((heads, nkeys, t), F32)
    out_b16 = jax.ShapeDtypeStruct((heads, nkeys, t), BF16)
    spec = pl.BlockSpec((heads, nkeys, tl), lambda i: (0, 0, i))
    return pl.pallas_call(
        functools.partial(_gate_kernel, topk=topk),
        out_shape=(out_f32, out_f32, out_b16, out_b16),
        grid=(t // tl,),
        in_specs=[pl.BlockSpec((n_sub, nkeys, tl), lambda i: (0, 0, i))],
        out_specs=(spec, spec, spec, spec),
        compiler_params=pltpu.CompilerParams(dimension_semantics=("arbitrary",),
                                             vmem_limit_bytes=VMEM_LIMIT_BYTES),
        name="peer_gate",
    )(sc)


def _peer_kernel(x1t_ref, x1_ref, u_ref, vt_ref, w1_ref, nsel_ref, w2_ref, rank_ref, g_ref, b_ref, o_ref,
                 acc_ref, act_ref, ga_ref, w2s_ref, ranks_ref, *, alpha, nkeys):
    e = pl.program_id(1)
    heads = w2_ref.shape[0]
    n_i = w1_ref.shape[1]
    tm = x1t_ref.shape[1]
    n_lt = tm // LANES

    @pl.when(e == 0)
    def _():
        acc_ref[...] = jnp.zeros_like(acc_ref)
        for c in range(n_lt):
            w2s_ref[:, c] = w2_ref[:, :, c * LANES:(c + 1) * LANES]
            ranks_ref[:, c] = rank_ref[:, :, c * LANES:(c + 1) * LANES]

    act = jnp.dot(u_ref[...], x1t_ref[...], preferred_element_type=F32)
    for c in range(n_lt):
        act_ref[c] = act[:, c * LANES:(c + 1) * LANES]

    def chunk(ils, j0, c):
        lanes = pl.ds(pl.multiple_of(c * LANES, LANES), LANES)
        keys = pl.ds(j0, CHUNK_ROWS)
        gates = [jnp.zeros((CHUNK_ROWS, LANES), BF16) for _ in ils]
        for h in range(heads):
            rank = ranks_ref[h, c, keys, :]
            w2 = w2s_ref[h, c, keys, :]
            for n, il in enumerate(ils):
                nsel = nsel_ref[h, il:il + 1, lanes].astype(BF16)
                w1 = w1_ref[h, il:il + 1, lanes].astype(BF16)
                gates[n] = gates[n] + jnp.where(rank < nsel, w2, jnp.zeros_like(w2)) * w1
        for n, il in enumerate(ils):
            rows = pl.ds(il * nkeys + j0, CHUNK_ROWS)
            a = act_ref[c, rows, :]
            gelu2 = a + a * lax.erf(a * math.sqrt(0.5))
            ga_ref[rows, lanes] = gates[n] * gelu2.astype(BF16)

    for il in range(0, n_i, CHUNK_KEYS):
        for j0 in range(0, nkeys, CHUNK_ROWS):
            def lane_tile(c, carry, ils=tuple(range(il, il + CHUNK_KEYS)), j0=j0):
                chunk(ils, j0, c)
                return carry
            lax.fori_loop(0, n_lt, lane_tile, 0)

    acc_ref[...] += jnp.dot(vt_ref[0], ga_ref[...], preferred_element_type=F32)

    @pl.when(e == pl.num_programs(1) - 1)
    def _():
        z = alpha * x1_ref[...] + acc_ref[...].T
        o_ref[...] = _layer_norm(z, g_ref[...], b_ref[...])


def _peer_dense(x1t, x1, u_b, vt_blk, w1, nsel, w2, rank, ln_g, ln_b, *, alpha, tm=1024):
    t, d = x1.shape
    n_eblk, _, be = vt_blk.shape
    heads, nkeys, _ = w2.shape
    n_i = be // nkeys
    n_lt = tm // LANES
    once = pl.Buffered(1)
    kern = functools.partial(_peer_kernel, alpha=alpha, nkeys=nkeys)
    return pl.pallas_call(
        kern,
        out_shape=jax.ShapeDtypeStruct((t, d), F32),
        grid=(t // tm, n_eblk),
        in_specs=[pl.BlockSpec((d, tm), lambda i, e: (0, i), pipeline_mode=once),
                  pl.BlockSpec((tm, d), lambda i, e: (i, 0), pipeline_mode=once),
                  pl.BlockSpec((be, d), lambda i, e: (e, 0)),
                  pl.BlockSpec((1, d, be), lambda i, e: (e, 0, 0)),
                  pl.BlockSpec((heads, n_i, tm), lambda i, e: (0, e, i)),
                  pl.BlockSpec((heads, n_i, tm), lambda i, e: (0, e, i)),
                  pl.BlockSpec((heads, nkeys, tm), lambda i, e: (0, 0, i), pipeline_mode=once),
                  pl.BlockSpec((heads, nkeys, tm), lambda i, e: (0, 0, i), pipeline_mode=once),
                  pl.BlockSpec(ln_g.shape, lambda i, e: (0, 0)),
                  pl.BlockSpec(ln_b.shape, lambda i, e: (0, 0))],
        out_specs=pl.BlockSpec((tm, d), lambda i, e: (i, 0), pipeline_mode=once),
        scratch_shapes=[pltpu.VMEM((d, tm), F32), pltpu.VMEM((n_lt, be, LANES), F32),
                        pltpu.VMEM((be, tm), BF16),
                        pltpu.VMEM((heads, n_lt, nkeys, LANES), BF16),
                        pltpu.VMEM((heads, n_lt, nkeys, LANES), BF16)],
        compiler_params=pltpu.CompilerParams(dimension_semantics=("arbitrary", "arbitrary"),
                                             vmem_limit_bytes=VMEM_LIMIT_BYTES),
        name="peer_dense",
    )(x1t, x1, u_b, vt_blk, w1, nsel, w2, rank, ln_g, ln_b)


def kernel(x, w_in, conv_w, conv_b, lam_q1, lam_k1, lam_q2, lam_k2, subln_g, w_out, ln1_g, ln1_b,
           peer_wq, peer_keys, peer_u, peer_v, ln2_g, ln2_b):
    batch, seq, d = x.shape
    depth = w_in.shape[0]
    d_conv = conv_w.shape[2]
    att_vdim = subln_g.shape[1]
    d_v = N_ATT_HEADS * att_vdim
    d_qk = (w_in.shape[2] - 3 * d_conv - d_v) // 2
    qk_dim = d_qk // (2 * N_ATT_HEADS)
    alpha = (2 * depth) ** 0.25
    heads, _, nkeys, dhalf = peer_keys.shape[1:]
    row = lambda a: a.reshape(1, -1)

    x2 = x.reshape(batch * seq, d)
    for l in range(depth):
        lam_init = 0.8 - 0.6 * math.exp(-0.3 * l)
        qscale = LOG2E / math.sqrt(qk_dim)
        yconv, q, k, v = _inproj_conv(x2, w_in[l].astype(BF16), conv_w[l], row(conv_b[l]), seq=seq,
                                      d_conv=d_conv, d_qk=d_qk, d_v=d_v, qscale=qscale)
        yatt = _diff_attn(q, k, v, row(lam_q1[l]), row(lam_k1[l]), row(lam_q2[l]), row(lam_k2[l]),
                          row(subln_g[l]), batch=batch, seq=seq, lam_init=lam_init)
        x1, x1t, sc = _outproj_query(x2, yconv, yatt, w_out[l].astype(BF16), row(ln1_g[l]), row(ln1_b[l]),
                                     peer_wq[l].astype(BF16),
                                     peer_keys[l].reshape(heads * 2, nkeys, dhalf).astype(BF16), alpha=alpha)
        w1, nsel, w2, rank = _peer_gate(sc, topk=PEER_TOPK)
        vt_blk = peer_v[l].reshape(-1, PEER_EXPERT_BLOCK, d).transpose(0, 2, 1).astype(BF16)
        x2 = _peer_dense(x1t, x1, peer_u[l].astype(BF16), vt_blk, w1, nsel, w2, rank,
                         row(ln2_g[l]), row(ln2_b[l]), alpha=alpha)
    return x2.reshape(batch, seq, d)
```

```python
import functools
import math

import jax
import jax.numpy as jnp
from jax import lax
from jax.experimental import pallas as pl
from jax.experimental.pallas import tpu as pltpu

F32 = jnp.float32
BF16 = jnp.bfloat16

CONV_WIDTH = 3
N_ATT_HEADS = 4
PEER_HEADS = 8
PEER_NKEYS = 128
PEER_TOPK = 16
PEER_EXPERT_BLOCK = 1024
CHUNK_KEYS = 8
CHUNK_ROWS = 128
LN_EPS = 1e-5
LOG2E = 1.4426950408889634
NEG_BIG = -1e30

SUBLANES = 8
LANES = 128
VMEM_LIMIT_BYTES = 56 * 1024 * 1024


def _nt_dot(a, b):
    return lax.dot_general(a, b, (((1,), (1,)), ((), ())), preferred_element_type=F32)


def _inproj_kernel(x_ref, w_ref, cw_ref, cb_ref, yconv_ref, q_ref, k_ref, vt_ref, carry_ref,
                   *, tiles_per_seq, d_conv, d_qk, n_heads, qscale):
    i = pl.program_id(0)

    @pl.when(i % tiles_per_seq == 0)
    def _():
        carry_ref[...] = jnp.zeros_like(carry_ref)

    xb = x_ref[...].astype(BF16)

    def proj(lo, width):
        return jnp.dot(xb, w_ref[:, lo:lo + width], preferred_element_type=F32)

    gb = proj(0, d_conv)
    u = proj(d_conv, d_conv) * proj(2 * d_conv, d_conv)
    tm = u.shape[0]
    row = lax.broadcasted_iota(jnp.int32, u.shape, 0)
    c1 = carry_ref[SUBLANES - 1:SUBLANES, :]
    c2 = carry_ref[SUBLANES - 2:SUBLANES - 1, :]
    u1 = jnp.where(row == 0, c1, pltpu.roll(u, 1, 0))
    u2 = jnp.where(row == 0, c2, jnp.where(row == 1, c1, pltpu.roll(u, 2, 0)))
    conv = cb_ref[...] + cw_ref[2:3, :] * u + cw_ref[1:2, :] * u1 + cw_ref[0:1, :] * u2
    carry_ref[...] = u[tm - SUBLANES:, :]
    yconv_ref[...] = (gb * conv).astype(yconv_ref.dtype)

    base = 3 * d_conv
    q = (proj(base, d_qk) * qscale).astype(q_ref.dtype)
    hq = d_qk // n_heads
    lane = lax.broadcasted_iota(jnp.int32, (tm, hq), 1)
    zero = jnp.zeros((tm, hq), q_ref.dtype)
    for h in range(n_heads):
        qh = q[:, h * hq:(h + 1) * hq]
        q_ref[:, (2 * h) * hq:(2 * h + 1) * hq] = jnp.where(lane < hq // 2, qh, zero)
        q_ref[:, (2 * h + 1) * hq:(2 * h + 2) * hq] = jnp.where(lane >= hq // 2, qh, zero)
    k_ref[...] = proj(base + d_qk, d_qk).astype(k_ref.dtype)
    vt_ref[...] = proj(base + 2 * d_qk, vt_ref.shape[0]).T.astype(vt_ref.dtype)


def _inproj_conv(x2, w_in_b, conv_w, conv_b, *, seq, d_conv, d_qk, d_v, qscale, tm=512):
    t, d = x2.shape
    assert seq % tm == 0
    kern = functools.partial(_inproj_kernel, tiles_per_seq=seq // tm, d_conv=d_conv, d_qk=d_qk,
                             n_heads=N_ATT_HEADS, qscale=qscale)
    return pl.pallas_call(
        kern,
        out_shape=(jax.ShapeDtypeStruct((t, d_conv), BF16), jax.ShapeDtypeStruct((t, 2 * d_qk), BF16),
                   jax.ShapeDtypeStruct((t, d_qk), BF16), jax.ShapeDtypeStruct((d_v, t), BF16)),
        grid=(t // tm,),
        in_specs=[pl.BlockSpec((tm, d), lambda i: (i, 0)),
                  pl.BlockSpec(w_in_b.shape, lambda i: (0, 0)),
                  pl.BlockSpec(conv_w.shape, lambda i: (0, 0)),
                  pl.BlockSpec(conv_b.shape, lambda i: (0, 0))],
        out_specs=(pl.BlockSpec((tm, d_conv), lambda i: (i, 0)), pl.BlockSpec((tm, 2 * d_qk), lambda i: (i, 0)),
                   pl.BlockSpec((tm, d_qk), lambda i: (i, 0)), pl.BlockSpec((d_v, tm), lambda i: (0, i))),
        scratch_shapes=[pltpu.VMEM((SUBLANES, d_conv), F32)],
        compiler_params=pltpu.CompilerParams(dimension_semantics=("arbitrary",),
                                             vmem_limit_bytes=VMEM_LIMIT_BYTES),
        name="inproj_conv",
    )(x2, w_in_b, conv_w, conv_b)


def _attn_kernel(lq1_ref, lk1_ref, lq2_ref, lk2_ref, g_ref, q_ref, k_ref, vt_ref, o_ref,
                 m_ref, l_ref, acc_ref, *, tq, n_heads, lam_init):
    qi = pl.program_id(1)
    hw = k_ref.shape[1] // n_heads
    n_chains = 2 * n_heads

    m_ref[...] = jnp.full(m_ref.shape, NEG_BIG, F32)
    l_ref[...] = jnp.zeros(l_ref.shape, F32)
    acc_ref[...] = jnp.zeros(acc_ref.shape, F32)

    def block(j, diagonal):
        off = pl.multiple_of(j * tq, tq)
        scores = [_nt_dot(k_ref[pl.ds(off, tq), (c // 2) * hw:(c // 2 + 1) * hw], q_ref[:, c * hw:(c + 1) * hw])
                  for c in range(n_chains)]
        probs, alphas = [], []
        for c in range(n_chains):
            s = scores[c]
            if diagonal:
                key = lax.broadcasted_iota(jnp.int32, s.shape, 0)
                qry = lax.broadcasted_iota(jnp.int32, s.shape, 1)
                s = jnp.where(key <= qry, s, NEG_BIG)
            m_old = m_ref[c]
            m_new = jnp.maximum(m_old, jnp.max(s, axis=0, keepdims=True))
            alpha = jnp.exp2(m_old - m_new)
            p = jnp.exp2(s - m_new)
            m_ref[c] = m_new
            l_ref[c] = alpha * l_ref[c] + jnp.sum(p, axis=0, keepdims=True)
            probs.append(p.astype(vt_ref.dtype))
            alphas.append(alpha)
        for c in range(n_chains):
            vt = vt_ref[(c // 2) * hw:(c // 2 + 1) * hw, pl.ds(off, tq)]
            acc_ref[c] = alphas[c] * acc_ref[c] + jnp.dot(vt, probs[c], preferred_element_type=F32)

    def body(j, carry):
        block(j, False)
        return carry

    lax.fori_loop(0, qi, body, 0)
    block(qi, True)

    lam = (jnp.exp(jnp.sum(lq1_ref[...] * lk1_ref[...], axis=1, keepdims=True))
           - jnp.exp(jnp.sum(lq2_ref[...] * lk2_ref[...], axis=1, keepdims=True)) + lam_init)
    for h in range(n_heads):
        o1 = acc_ref[2 * h] / l_ref[2 * h]
        o2 = acc_ref[2 * h + 1] / l_ref[2 * h + 1]
        y = o1 - lam * o2
        y = y * lax.rsqrt(jnp.mean(y * y, axis=0, keepdims=True) + LN_EPS) * g_ref[...]
        o_ref[:, h * hw:(h + 1) * hw] = (y * (1.0 - lam_init)).T.astype(o_ref.dtype)


def _diff_attn(q, k, vt, lam_q1, lam_k1, lam_q2, lam_k2, subln_g_col, *, batch, seq, lam_init, tq=512):
    t, d_k = k.shape
    d_v = vt.shape[0]
    n_heads = N_ATT_HEADS
    assert d_k == d_v and q.shape[1] == 2 * d_k
    nq = seq // tq
    kern = functools.partial(_attn_kernel, tq=tq, n_heads=n_heads, lam_init=lam_init)
    small = lambda a: pl.BlockSpec(a.shape, lambda b, i: (0, 0))
    n_chains = 2 * n_heads
    return pl.pallas_call(
        kern,
        out_shape=jax.ShapeDtypeStruct((t, d_v), BF16),
        grid=(batch, nq),
        in_specs=[small(lam_q1), small(lam_k1), small(lam_q2), small(lam_k2), small(subln_g_col),
                  pl.BlockSpec((tq, 2 * d_k), lambda b, i: (b * nq + i, 0)),
                  pl.BlockSpec((seq, d_k), lambda b, i: (b, 0)),
                  pl.BlockSpec((d_v, seq), lambda b, i: (0, b))],
        out_specs=pl.BlockSpec((tq, d_v), lambda b, i: (b * nq + i, 0)),
        scratch_shapes=[pltpu.VMEM((n_chains, 1, tq), F32), pltpu.VMEM((n_chains, 1, tq), F32),
                        pltpu.VMEM((n_chains, d_v // n_heads, tq), F32)],
        compiler_params=pltpu.CompilerParams(dimension_semantics=("arbitrary", "arbitrary"),
                                             vmem_limit_bytes=VMEM_LIMIT_BYTES),
        name="diff_attn",
    )(lam_q1, lam_k1, lam_q2, lam_k2, subln_g_col, q, k, vt)


def _layer_norm(z, g, b):
    mu = jnp.mean(z, axis=1, keepdims=True)
    zc = z - mu
    var = jnp.mean(zc * zc, axis=1, keepdims=True)
    return zc * lax.rsqrt(var + LN_EPS) * g + b


def _outproj_kernel(x_ref, yc_ref, ya_ref, wo_ref, g_ref, b_ref, wq_ref, keys_ref,
                    x1_ref, x1t_ref, sc_ref, *, alpha, d_conv):
    ymix = (jnp.dot(yc_ref[...], wo_ref[:d_conv, :], preferred_element_type=F32)
            + jnp.dot(ya_ref[...], wo_ref[d_conv:, :], preferred_element_type=F32))
    x1 = _layer_norm(alpha * x_ref[...] + ymix, g_ref[...], b_ref[...])
    x1_ref[...] = x1
    x1t_ref[...] = x1.T.astype(BF16)
    n_sub, _, dh = keys_ref.shape
    q = jnp.dot(x1.astype(BF16), wq_ref[...], preferred_element_type=F32).astype(BF16)
    for s in range(n_sub):
        sc_ref[s] = _nt_dot(keys_ref[s], q[:, s * dh:(s + 1) * dh])


def _outproj_query(x2, yconv, yatt, w_out_b, ln_g, ln_b, wq_b, keys_b, *, alpha, tm=512):
    t, d = x2.shape
    d_conv = yconv.shape[1]
    n_sub, nkeys, _ = keys_b.shape
    kern = functools.partial(_outproj_kernel, alpha=alpha, d_conv=d_conv)
    full = lambda a: pl.BlockSpec(a.shape, lambda i: (0,) * a.ndim)
    return pl.pallas_call(
        kern,
        out_shape=(jax.ShapeDtypeStruct((t, d), F32), jax.ShapeDtypeStruct((d, t), BF16),
                   jax.ShapeDtypeStruct((n_sub, nkeys, t), F32)),
        grid=(t // tm,),
        in_specs=[pl.BlockSpec((tm, d), lambda i: (i, 0)),
                  pl.BlockSpec((tm, d_conv), lambda i: (i, 0)),
                  pl.BlockSpec((tm, yatt.shape[1]), lambda i: (i, 0)),
                  full(w_out_b), full(ln_g), full(ln_b), full(wq_b), full(keys_b)],
        out_specs=(pl.BlockSpec((tm, d), lambda i: (i, 0)), pl.BlockSpec((d, tm), lambda i: (0, i)),
                   pl.BlockSpec((n_sub, nkeys, tm), lambda i: (0, 0, i))),
        compiler_params=pltpu.CompilerParams(dimension_semantics=("arbitrary",),
                                             vmem_limit_bytes=VMEM_LIMIT_BYTES),
        name="outproj_query",
    )(x2, yconv, yatt, w_out_b, ln_g, ln_b, wq_b, keys_b)


def _oddeven_merge_sort_pairs(n):
    pairs = []
    p = 1
    while p < n:
        k = p
        while k >= 1:
            for j in range(k % p, n - k, 2 * k):
                for i in range(min(k, n - j - k)):
                    if (i + j) // (2 * p) == (i + j + k) // (2 * p):
                        pairs.append((i + j, i + j + k))
            k //= 2
        p *= 2
    return pairs


def _sort_desc(vals):
    vals = list(vals)
    for i, j in _oddeven_merge_sort_pairs(len(vals)):
        hi = jnp.maximum(vals[i], vals[j])
        lo = jnp.minimum(vals[i], vals[j])
        vals[i], vals[j] = hi, lo
    return vals


def _bitonic_merge_desc(vals):
    vals = list(vals)
    n = len(vals)
    d = n // 2
    while d >= 1:
        for i in range(n):
            if i & d == 0:
                hi = jnp.maximum(vals[i], vals[i + d])
                lo = jnp.minimum(vals[i], vals[i + d])
                vals[i], vals[i + d] = hi, lo
        d //= 2
    return vals


def _top_sorted(s, k):
    n = s.shape[0]
    assert n == k * SUBLANES
    slabs = _sort_desc([s[a * SUBLANES:(a + 1) * SUBLANES, :] for a in range(k)])
    shift = SUBLANES // 2
    while shift >= 1:
        other = [pltpu.roll(x, shift, 0) for x in slabs]
        slabs = _bitonic_merge_desc([jnp.maximum(slabs[r], other[k - 1 - r]) for r in range(k)])
        shift //= 2
    return [x[0:1, :] for x in slabs]


def _next_below(s, kth, k):
    ge = s >= kth
    cnt = jnp.sum(jnp.where(ge, 1.0, 0.0), axis=0, keepdims=True)
    below = jnp.max(jnp.where(ge, NEG_BIG, s), axis=0, keepdims=True)
    return jnp.where(cnt > k, kth, below)


def _count_above(tops, x, *, strict):
    k = len(tops)
    levels = k.bit_length() - 1
    assert k == 1 << levels
    below = (lambda a, p: a < p) if strict else (lambda a, p: a <= p)
    conds = []

    def pivot(level, i, base):
        if i == level:
            return tops[base + (k >> (level + 1)) - 1]
        return jnp.where(conds[i], pivot(level, i + 1, base + (k >> (i + 1))), pivot(level, i + 1, base))

    for level in range(levels):
        conds.append(below(x, pivot(level, 0, 0)))
    count = jnp.zeros_like(x)
    for i, c in enumerate(conds):
        count = count + jnp.where(c, float(k >> (i + 1)), 0.0)
    return jnp.where(below(x, tops[k - 1]), float(k), count)


def _gate_kernel(sc_ref, w1_ref, nsel_ref, w2_ref, rank_ref, *, topk):
    n_sub = sc_ref.shape[0]
    heads = n_sub // 2
    k = topk
    tops = []
    for s_idx in range(n_sub):
        s = sc_ref[s_idx]
        rows = _top_sorted(s, k)
        rows.append(_next_below(s, rows[k - 1], k))
        tops.append([jnp.exp(r - rows[0]) for r in rows])
    a = [jnp.concatenate([tops[2 * h][r] for h in range(heads)], axis=0) for r in range(k + 1)]
    b = [jnp.concatenate([tops[2 * h + 1][r] for h in range(heads)], axis=0) for r in range(k + 1)]
    cands = [a[r] * b[c] for r in range(k + 1) for c in range(k + 1) if (r + 1) * (c + 1) <= k + 1]
    n_pad = 1
    while n_pad < len(cands):
        n_pad *= 2
    pad = jnp.full_like(cands[0], -1.0)
    srt = _sort_desc(cands + [pad] * (n_pad - len(cands)))
    tstar = 0.5 * (srt[k - 1] + srt[k])
    z = jnp.zeros_like(tstar)
    for c in cands:
        z = z + jnp.where(c >= tstar, c, 0.0)
    zinv = 1.0 / z
    for h in range(heads):
        e1 = jnp.exp(sc_ref[2 * h] - jnp.max(sc_ref[2 * h], axis=0, keepdims=True))
        e2 = jnp.exp(sc_ref[2 * h + 1] - jnp.max(sc_ref[2 * h + 1], axis=0, keepdims=True))
        theta = tstar[h:h + 1, :] / e1
        top2 = tops[2 * h + 1][:k]
        rank = _count_above(top2, e2, strict=True)
        nsel = _count_above(top2, theta, strict=False)
        w1_ref[h] = e1
        nsel_ref[h] = nsel
        w2_ref[h] = (e2 * (0.5 * zinv[h:h + 1, :])).astype(w2_ref.dtype)
        rank_ref[h] = rank.astype(rank_ref.dtype)


def _peer_gate(sc, *, topk, tl=256):
    n_sub, nkeys, t = sc.shape
    heads = n_sub // 2
    out_f32 = jax.ShapeDtypeStruct((heads, nkeys, t), F32)
    out_b16 = jax.ShapeDtypeStruct((heads, nkeys, t), BF16)
    spec = pl.BlockSpec((heads, nkeys, tl), lambda i: (0, 0, i))
    return pl.pallas_call(
        functools.partial(_gate_kernel, topk=topk),
        out_shape=(out_f32, out_f32, out_b16, out_b16),
        grid=(t // tl,),
        in_specs=[pl.BlockSpec((n_sub, nkeys, tl), lambda i: (0, 0, i))],
        out_specs=(spec, spec, spec, spec),
        compiler_params=pltpu.CompilerParams(dimension_semantics=("arbitrary",),
                                             vmem_limit_bytes=VMEM_LIMIT_BYTES),
        name="peer_gate",
    )(sc)


def _peer_kernel(x1t_ref, x1_ref, u_ref, vt_ref, w1_ref, nsel_ref, w2_ref, rank_ref, g_ref, b_ref, o_ref,
                 acc_ref, act_ref, ga_ref, w2s_ref, ranks_ref, *, alpha, nkeys):
    e = pl.program_id(1)
    heads = w2_ref.shape[0]
    n_i = w1_ref.shape[1]
    tm = x1t_ref.shape[1]
    n_lt = tm // LANES

    @pl.when(e == 0)
    def _():
        acc_ref[...] = jnp.zeros_like(acc_ref)
        for c in range(n_lt):
            w2s_ref[:, c] = w2_ref[:, :, c * LANES:(c + 1) * LANES]
            ranks_ref[:, c] = rank_ref[:, :, c * LANES:(c + 1) * LANES]

    act = jnp.dot(u_ref[...], x1t_ref[...], preferred_element_type=F32)
    for c in range(n_lt):
        act_ref[c] = act[:, c * LANES:(c + 1) * LANES]

    def chunk(ils, j0, c):
        lanes = pl.ds(pl.multiple_of(c * LANES, LANES), LANES)
        keys = pl.ds(j0, CHUNK_ROWS)
        gates = [jnp.zeros((CHUNK_ROWS, LANES), BF16) for _ in ils]
        for h in range(heads):
            rank = ranks_ref[h, c, keys, :]
            w2 = w2s_ref[h, c, keys, :]
            for n, il in enumerate(ils):
                nsel = nsel_ref[h, il:il + 1, lanes].astype(BF16)
                w1 = w1_ref[h, il:il + 1, lanes].astype(BF16)
                gates[n] = gates[n] + jnp.where(rank < nsel, w2, jnp.zeros_like(w2)) * w1
        for n, il in enumerate(ils):
            rows = pl.ds(il * nkeys + j0, CHUNK_ROWS)
            a = act_ref[c, rows, :]
            gelu2 = a + a * lax.erf(a * math.sqrt(0.5))
            ga_ref[rows, lanes] = gates[n] * gelu2.astype(BF16)

    for il in range(0, n_i, CHUNK_KEYS):
        for j0 in range(0, nkeys, CHUNK_ROWS):
            def lane_tile(c, carry, ils=tuple(range(il, il + CHUNK_KEYS)), j0=j0):
                chunk(ils, j0, c)
                return carry
            lax.fori_loop(0, n_lt, lane_tile, 0)

    acc_ref[...] += jnp.dot(vt_ref[0], ga_ref[...], preferred_element_type=F32)

    @pl.when(e == pl.num_programs(1) - 1)
    def _():
        z = alpha * x1_ref[...] + acc_ref[...].T
        o_ref[...] = _layer_norm(z, g_ref[...], b_ref[...])


def _peer_dense(x1t, x1, u_b, vt_blk, w1, nsel, w2, rank, ln_g, ln_b, *, alpha, tm=1024):
    t, d = x1.shape
    n_eblk, _, be = vt_blk.shape
    heads, nkeys, _ = w2.shape
    n_i = be // nkeys
    n_lt = tm // LANES
    once = pl.Buffered(1)
    kern = functools.partial(_peer_kernel, alpha=alpha, nkeys=nkeys)
    return pl.pallas_call(
        kern,
        out_shape=jax.ShapeDtypeStruct((t, d), F32),
        grid=(t // tm, n_eblk),
        in_specs=[pl.BlockSpec((d, tm), lambda i, e: (0, i), pipeline_mode=once),
                  pl.BlockSpec((tm, d), lambda i, e: (i, 0), pipeline_mode=once),
                  pl.BlockSpec((be, d), lambda i, e: (e, 0)),
                  pl.BlockSpec((1, d, be), lambda i, e: (e, 0, 0)),
                  pl.BlockSpec((heads, n_i, tm), lambda i, e: (0, e, i)),
                  pl.BlockSpec((heads, n_i, tm), lambda i, e: (0, e, i)),
                  pl.BlockSpec((heads, nkeys, tm), lambda i, e: (0, 0, i), pipeline_mode=once),
                  pl.BlockSpec((heads, nkeys, tm), lambda i, e: (0, 0, i), pipeline_mode=once),
                  pl.BlockSpec(ln_g.shape, lambda i, e: (0, 0)),
                  pl.BlockSpec(ln_b.shape, lambda i, e: (0, 0))],
        out_specs=pl.BlockSpec((tm, d), lambda i, e: (i, 0), pipeline_mode=once),
        scratch_shapes=[pltpu.VMEM((d, tm), F32), pltpu.VMEM((n_lt, be, LANES), F32),
                        pltpu.VMEM((be, tm), BF16),
                        pltpu.VMEM((heads, n_lt, nkeys, LANES), BF16),
                        pltpu.VMEM((heads, n_lt, nkeys, LANES), BF16)],
        compiler_params=pltpu.CompilerParams(dimension_semantics=("arbitrary", "arbitrary"),
                                             vmem_limit_bytes=VMEM_LIMIT_BYTES),
        name="peer_dense",
    )(x1t, x1, u_b, vt_blk, w1, nsel, w2, rank, ln_g, ln_b)


def kernel(x, w_in, conv_w, conv_b, lam_q1, lam_k1, lam_q2, lam_k2, subln_g, w_out, ln1_g, ln1_b,
           peer_wq, peer_keys, peer_u, peer_v, ln2_g, ln2_b):
    batch, seq, d = x.shape
    depth = w_in.shape[0]
    d_conv = conv_w.shape[2]
    att_vdim = subln_g.shape[1]
    d_v = N_ATT_HEADS * att_vdim
    d_qk = (w_in.shape[2] - 3 * d_conv - d_v) // 2
    qk_dim = d_qk // (2 * N_ATT_HEADS)
    alpha = (2 * depth) ** 0.25
    heads, _, nkeys, dhalf = peer_keys.shape[1:]
    row = lambda a: a.reshape(1, -1)

    x2 = x.reshape(batch * seq, d)
    for l in range(depth):
        lam_init = 0.8 - 0.6 * math.exp(-0.3 * l)
        qscale = LOG2E / math.sqrt(qk_dim)
        yconv, q, k, vt = _inproj_conv(x2, w_in[l].astype(BF16), conv_w[l], row(conv_b[l]), seq=seq,
                                      d_conv=d_conv, d_qk=d_qk, d_v=d_v, qscale=qscale)
        yatt = _diff_attn(q, k, vt, row(lam_q1[l]), row(lam_k1[l]), row(lam_q2[l]), row(lam_k2[l]),
                          subln_g[l].reshape(-1, 1), batch=batch, seq=seq, lam_init=lam_init)
        x1, x1t, sc = _outproj_query(x2, yconv, yatt, w_out[l].astype(BF16), row(ln1_g[l]), row(ln1_b[l]),
                                     peer_wq[l].astype(BF16),
                                     peer_keys[l].reshape(heads * 2, nkeys, dhalf).astype(BF16), alpha=alpha)
        w1, nsel, w2, rank = _peer_gate(sc, topk=PEER_TOPK)
        vt_blk = peer_v[l].reshape(-1, PEER_EXPERT_BLOCK, d).transpose(0, 2, 1).astype(BF16)
        x2 = _peer_dense(x1t, x1, peer_u[l].astype(BF16), vt_blk, w1, nsel, w2, rank,
                         row(ln2_g[l]), row(ln2_b[l]), alpha=alpha)
    return x2.reshape(batch, seq, d)
```

```python
import functools
import math

import jax
import jax.numpy as jnp
from jax import lax
from jax.experimental import pallas as pl
from jax.experimental.pallas import tpu as pltpu

F32 = jnp.float32
BF16 = jnp.bfloat16

CONV_WIDTH = 3
N_ATT_HEADS = 4
PEER_HEADS = 8
PEER_NKEYS = 128
PEER_TOPK = 16
PEER_EXPERT_BLOCK = 1024
CHUNK_KEYS = 8
CHUNK_ROWS = 128
LN_EPS = 1e-5
LOG2E = 1.4426950408889634
NEG_BIG = -1e30

SUBLANES = 8
LANES = 128
VMEM_LIMIT_BYTES = 60 * 1024 * 1024


def _nt_dot(a, b):
    return lax.dot_general(a, b, (((1,), (1,)), ((), ())), preferred_element_type=F32)


def _inproj_kernel(x_ref, w_ref, cw_ref, cb_ref, yconv_ref, q_ref, k_ref, v_ref, carry_ref,
                   *, tiles_per_seq, d_conv, d_qk, n_heads, qscale):
    i = pl.program_id(0)

    @pl.when(i % tiles_per_seq == 0)
    def _():
        carry_ref[...] = jnp.zeros_like(carry_ref)

    xb = x_ref[...].astype(BF16)

    def proj(lo, width):
        return jnp.dot(xb, w_ref[:, lo:lo + width], preferred_element_type=F32)

    gb = proj(0, d_conv)
    u = proj(d_conv, d_conv) * proj(2 * d_conv, d_conv)
    tm = u.shape[0]
    row = lax.broadcasted_iota(jnp.int32, u.shape, 0)
    c1 = carry_ref[SUBLANES - 1:SUBLANES, :]
    c2 = carry_ref[SUBLANES - 2:SUBLANES - 1, :]
    u1 = jnp.where(row == 0, c1, pltpu.roll(u, 1, 0))
    u2 = jnp.where(row == 0, c2, jnp.where(row == 1, c1, pltpu.roll(u, 2, 0)))
    conv = cb_ref[...] + cw_ref[2:3, :] * u + cw_ref[1:2, :] * u1 + cw_ref[0:1, :] * u2
    carry_ref[...] = u[tm - SUBLANES:, :]
    yconv_ref[...] = (gb * conv).astype(yconv_ref.dtype)

    base = 3 * d_conv
    q = (proj(base, d_qk) * qscale).astype(q_ref.dtype)
    hq = d_qk // n_heads
    lane = lax.broadcasted_iota(jnp.int32, (tm, hq), 1)
    zero = jnp.zeros((tm, hq), q_ref.dtype)
    for h in range(n_heads):
        qh = q[:, h * hq:(h + 1) * hq]
        q_ref[:, (2 * h) * hq:(2 * h + 1) * hq] = jnp.where(lane < hq // 2, qh, zero)
        q_ref[:, (2 * h + 1) * hq:(2 * h + 2) * hq] = jnp.where(lane >= hq // 2, qh, zero)
    k_ref[...] = proj(base + d_qk, d_qk).astype(k_ref.dtype)
    v_ref[...] = proj(base + 2 * d_qk, v_ref.shape[1]).astype(v_ref.dtype)


def _inproj_conv(x2, w_in_b, conv_w, conv_b, *, seq, d_conv, d_qk, d_v, qscale, tm=512):
    t, d = x2.shape
    assert seq % tm == 0
    kern = functools.partial(_inproj_kernel, tiles_per_seq=seq // tm, d_conv=d_conv, d_qk=d_qk,
                             n_heads=N_ATT_HEADS, qscale=qscale)
    return pl.pallas_call(
        kern,
        out_shape=(jax.ShapeDtypeStruct((t, d_conv), BF16), jax.ShapeDtypeStruct((t, 2 * d_qk), BF16),
                   jax.ShapeDtypeStruct((t, d_qk), BF16), jax.ShapeDtypeStruct((t, d_v), BF16)),
        grid=(t // tm,),
        in_specs=[pl.BlockSpec((tm, d), lambda i: (i, 0)),
                  pl.BlockSpec(w_in_b.shape, lambda i: (0, 0)),
                  pl.BlockSpec(conv_w.shape, lambda i: (0, 0)),
                  pl.BlockSpec(conv_b.shape, lambda i: (0, 0))],
        out_specs=(pl.BlockSpec((tm, d_conv), lambda i: (i, 0)), pl.BlockSpec((tm, 2 * d_qk), lambda i: (i, 0)),
                   pl.BlockSpec((tm, d_qk), lambda i: (i, 0)), pl.BlockSpec((tm, d_v), lambda i: (i, 0))),
        scratch_shapes=[pltpu.VMEM((SUBLANES, d_conv), F32)],
        compiler_params=pltpu.CompilerParams(dimension_semantics=("arbitrary",),
                                             vmem_limit_bytes=VMEM_LIMIT_BYTES),
        name="inproj_conv",
    )(x2, w_in_b, conv_w, conv_b)


def _attn_kernel(lq1_ref, lk1_ref, lq2_ref, lk2_ref, g_ref, q_ref, k_ref, v_ref, o_ref,
                 m_ref, l_ref, acc_ref, *, tq, n_heads, lam_init):
    qi = pl.program_id(1)
    hw = k_ref.shape[1] // n_heads
    n_chains = 2 * n_heads

    m_ref[...] = jnp.full(m_ref.shape, NEG_BIG, F32)
    l_ref[...] = jnp.zeros(l_ref.shape, F32)
    acc_ref[...] = jnp.zeros(acc_ref.shape, F32)

    def block(j, diagonal):
        off = pl.multiple_of(j * tq, tq)
        n_slab = tq // LANES
        scores = [_nt_dot(q_ref[:, c * hw:(c + 1) * hw], k_ref[pl.ds(off, tq), (c // 2) * hw:(c // 2 + 1) * hw])
                  for c in range(n_chains)]
        probs, alphas = [], []
        for c in range(n_chains):
            slabs = [scores[c][:, t * LANES:(t + 1) * LANES] for t in range(n_slab)]
            if diagonal:
                r = lax.broadcasted_iota(jnp.int32, (tq, LANES), 0)
                col = lax.broadcasted_iota(jnp.int32, (tq, LANES), 1)
                slabs = [jnp.where(col + t * LANES <= r, sl, NEG_BIG) for t, sl in enumerate(slabs)]
            mx = slabs[0]
            for sl in slabs[1:]:
                mx = jnp.maximum(mx, sl)
            m_old = m_ref[c]
            m_new = jnp.maximum(m_old, jnp.broadcast_to(jnp.max(mx, axis=1, keepdims=True), m_old.shape))
            alpha = jnp.exp2(m_old - m_new)
            ps = [jnp.exp2(sl - m_new) for sl in slabs]
            part = ps[0]
            for p in ps[1:]:
                part = part + p
            m_ref[c] = m_new
            l_ref[c] = alpha * l_ref[c] + part
            probs.append(jnp.concatenate(ps, axis=1).astype(v_ref.dtype))
            alphas.append(alpha)
        for c in range(n_chains):
            vc = v_ref[pl.ds(off, tq), (c // 2) * hw:(c // 2 + 1) * hw]
            acc_ref[c] = alphas[c] * acc_ref[c] + jnp.dot(probs[c], vc, preferred_element_type=F32)

    def body(j, carry):
        block(j, False)
        return carry

    lax.fori_loop(0, qi, body, 0)
    block(qi, True)

    lam = (jnp.exp(jnp.sum(lq1_ref[...] * lk1_ref[...], axis=1, keepdims=True))
           - jnp.exp(jnp.sum(lq2_ref[...] * lk2_ref[...], axis=1, keepdims=True)) + lam_init)
    for h in range(n_heads):
        o1 = acc_ref[2 * h] / jnp.sum(l_ref[2 * h], axis=1, keepdims=True)
        o2 = acc_ref[2 * h + 1] / jnp.sum(l_ref[2 * h + 1], axis=1, keepdims=True)
        y = o1 - lam * o2
        y = y * lax.rsqrt(jnp.mean(y * y, axis=1, keepdims=True) + LN_EPS) * g_ref[...]
        o_ref[:, h * hw:(h + 1) * hw] = (y * (1.0 - lam_init)).astype(o_ref.dtype)


def _diff_attn(q, k, v, lam_q1, lam_k1, lam_q2, lam_k2, subln_g, *, batch, seq, lam_init, tq=512):
    t, d_k = k.shape
    d_v = v.shape[1]
    n_heads = N_ATT_HEADS
    assert d_k == d_v and q.shape[1] == 2 * d_k
    nq = seq // tq
    kern = functools.partial(_attn_kernel, tq=tq, n_heads=n_heads, lam_init=lam_init)
    small = lambda a: pl.BlockSpec(a.shape, lambda b, i: (0, 0))
    n_chains = 2 * n_heads
    return pl.pallas_call(
        kern,
        out_shape=jax.ShapeDtypeStruct((t, d_v), BF16),
        grid=(batch, nq),
        in_specs=[small(lam_q1), small(lam_k1), small(lam_q2), small(lam_k2), small(subln_g),
                  pl.BlockSpec((tq, 2 * d_k), lambda b, i: (b * nq + i, 0)),
                  pl.BlockSpec((seq, d_k), lambda b, i: (b, 0)),
                  pl.BlockSpec((seq, d_v), lambda b, i: (b, 0))],
        out_specs=pl.BlockSpec((tq, d_v), lambda b, i: (b * nq + i, 0)),
        scratch_shapes=[pltpu.VMEM((n_chains, tq, LANES), F32), pltpu.VMEM((n_chains, tq, LANES), F32),
                        pltpu.VMEM((n_chains, tq, d_v // n_heads), F32)],
        compiler_params=pltpu.CompilerParams(dimension_semantics=("arbitrary", "arbitrary"),
                                             vmem_limit_bytes=VMEM_LIMIT_BYTES),
        name="diff_attn",
    )(lam_q1, lam_k1, lam_q2, lam_k2, subln_g, q, k, v)


def _layer_norm(z, g, b):
    mu = jnp.mean(z, axis=1, keepdims=True)
    zc = z - mu
    var = jnp.mean(zc * zc, axis=1, keepdims=True)
    return zc * lax.rsqrt(var + LN_EPS) * g + b


def _outproj_kernel(x_ref, yc_ref, ya_ref, wo_ref, g_ref, b_ref, wq_ref, keys_ref,
                    x1_ref, x1t_ref, sc_ref, *, alpha, d_conv):
    ymix = (jnp.dot(yc_ref[...], wo_ref[:d_conv, :], preferred_element_type=F32)
            + jnp.dot(ya_ref[...], wo_ref[d_conv:, :], preferred_element_type=F32))
    x1 = _layer_norm(alpha * x_ref[...] + ymix, g_ref[...], b_ref[...])
    x1_ref[...] = x1
    x1t_ref[...] = x1.T.astype(BF16)
    n_sub, _, dh = keys_ref.shape
    q = jnp.dot(x1.astype(BF16), wq_ref[...], preferred_element_type=F32).astype(BF16)
    for s in range(n_sub):
        sc_ref[s] = _nt_dot(keys_ref[s], q[:, s * dh:(s + 1) * dh])


def _outproj_query(x2, yconv, yatt, w_out_b, ln_g, ln_b, wq_b, keys_b, *, alpha, tm=512):
    t, d = x2.shape
    d_conv = yconv.shape[1]
    n_sub, nkeys, _ = keys_b.shape
    kern = functools.partial(_outproj_kernel, alpha=alpha, d_conv=d_conv)
    full = lambda a: pl.BlockSpec(a.shape, lambda i: (0,) * a.ndim)
    return pl.pallas_call(
        kern,
        out_shape=(jax.ShapeDtypeStruct((t, d), F32), jax.ShapeDtypeStruct((d, t), BF16),
                   jax.ShapeDtypeStruct((n_sub, nkeys, t), F32)),
        grid=(t // tm,),
        in_specs=[pl.BlockSpec((tm, d), lambda i: (i, 0)),
                  pl.BlockSpec((tm, d_conv), lambda i: (i, 0)),
                  pl.BlockSpec((tm, yatt.shape[1]), lambda i: (i, 0)),
                  full(w_out_b), full(ln_g), full(ln_b), full(wq_b), full(keys_b)],
        out_specs=(pl.BlockSpec((tm, d), lambda i: (i, 0)), pl.BlockSpec((d, tm), lambda i: (0, i)),
                   pl.BlockSpec((n_sub, nkeys, tm), lambda i: (0, 0, i))),
        compiler_params=pltpu.CompilerParams(dimension_semantics=("arbitrary",),
                                             vmem_limit_bytes=VMEM_LIMIT_BYTES),
        name="outproj_query",
    )(x2, yconv, yatt, w_out_b, ln_g, ln_b, wq_b, keys_b)


def _oddeven_merge_sort_pairs(n):
    pairs = []
    p = 1
    while p < n:
        k = p
        while k >= 1:
            for j in range(k % p, n - k, 2 * k):
                for i in range(min(k, n - j - k)):
                    if (i + j) // (2 * p) == (i + j + k) // (2 * p):
                        pairs.append((i + j, i + j + k))
            k //= 2
        p *= 2
    return pairs


def _sort_desc(vals):
    vals = list(vals)
    for i, j in _oddeven_merge_sort_pairs(len(vals)):
        hi = jnp.maximum(vals[i], vals[j])
        lo = jnp.minimum(vals[i], vals[j])
        vals[i], vals[j] = hi, lo
    return vals


def _bitonic_merge_desc(vals):
    vals = list(vals)
    n = len(vals)
    d = n // 2
    while d >= 1:
        for i in range(n):
            if i & d == 0:
                hi = jnp.maximum(vals[i], vals[i + d])
                lo = jnp.minimum(vals[i], vals[i + d])
                vals[i], vals[i + d] = hi, lo
        d //= 2
    return vals


def _top_sorted(s, k):
    n = s.shape[0]
    assert n == k * SUBLANES
    slabs = _sort_desc([s[a * SUBLANES:(a + 1) * SUBLANES, :] for a in range(k)])
    shift = SUBLANES // 2
    while shift >= 1:
        other = [pltpu.roll(x, shift, 0) for x in slabs]
        slabs = _bitonic_merge_desc([jnp.maximum(slabs[r], other[k - 1 - r]) for r in range(k)])
        shift //= 2
    return [x[0:1, :] for x in slabs]


def _next_below(s, kth, k):
    ge = s >= kth
    cnt = jnp.sum(jnp.where(ge, 1.0, 0.0), axis=0, keepdims=True)
    below = jnp.max(jnp.where(ge, NEG_BIG, s), axis=0, keepdims=True)
    return jnp.where(cnt > k, kth, below)


def _count_above(tops, x, *, strict):
    k = len(tops)
    levels = k.bit_length() - 1
    assert k == 1 << levels
    below = (lambda a, p: a < p) if strict else (lambda a, p: a <= p)
    conds = []

    def pivot(level, i, base):
        if i == level:
            return tops[base + (k >> (level + 1)) - 1]
        return jnp.where(conds[i], pivot(level, i + 1, base + (k >> (i + 1))), pivot(level, i + 1, base))

    for level in range(levels):
        conds.append(below(x, pivot(level, 0, 0)))
    count = jnp.zeros_like(x)
    for i, c in enumerate(conds):
        count = count + jnp.where(c, float(k >> (i + 1)), 0.0)
    return jnp.where(below(x, tops[k - 1]), float(k), count)


def _gate_kernel(sc_ref, w1_ref, nsel_ref, w2_ref, rank_ref, *, topk):
    n_sub = sc_ref.shape[0]
    heads = n_sub // 2
    k = topk
    tops = []
    for s_idx in range(n_sub):
        s = sc_ref[s_idx]
        rows = _top_sorted(s, k)
        rows.append(_next_below(s, rows[k - 1], k))
        tops.append([jnp.exp(r - rows[0]) for r in rows])
    a = [jnp.concatenate([tops[2 * h][r] for h in range(heads)], axis=0) for r in range(k + 1)]
    b = [jnp.concatenate([tops[2 * h + 1][r] for h in range(heads)], axis=0) for r in range(k + 1)]
    cands = [a[r] * b[c] for r in range(k + 1) for c in range(k + 1) if (r + 1) * (c + 1) <= k + 1]
    n_pad = 1
    while n_pad < len(cands):
        n_pad *= 2
    pad = jnp.full_like(cands[0], -1.0)
    srt = _sort_desc(cands + [pad] * (n_pad - len(cands)))
    tstar = 0.5 * (srt[k - 1] + srt[k])
    z = jnp.zeros_like(tstar)
    for c in cands:
        z = z + jnp.where(c >= tstar, c, 0.0)
    zinv = 1.0 / z
    for h in range(heads):
        e1 = jnp.exp(sc_ref[2 * h] - jnp.max(sc_ref[2 * h], axis=0, keepdims=True))
        e2 = jnp.exp(sc_ref[2 * h + 1] - jnp.max(sc_ref[2 * h + 1], axis=0, keepdims=True))
        theta = tstar[h:h + 1, :] / e1
        top2 = tops[2 * h + 1][:k]
        rank = _count_above(top2, e2, strict=True)
        nsel = _count_above(top2, theta, strict=False)
        w1_ref[h] = e1
        nsel_ref[h] = nsel
        w2_ref[h] = (e2 * (0.5 * zinv[h:h + 1, :])).astype(w2_ref.dtype)
        rank_ref[h] = rank.astype(rank_ref.dtype)


def _peer_gate(sc, *, topk, tl=256):
    n_sub, nkeys, t = sc.shape
    heads = n_sub // 2
    out_f32 = jax.ShapeDtypeStruct((heads, nkeys, t), F32)
    out_b16 = jax.ShapeDtypeStruct((heads, nkeys, t), BF16)
    spec = pl.BlockSpec((heads, nkeys, tl), lambda i: (0, 0, i))
    return pl.pallas_call(
        functools.partial(_gate_kernel, topk=topk),
        out_shape=(out_f32, out_f32, out_b16, out_b16),
        grid=(t // tl,),
        in_specs=[pl.BlockSpec((n_sub, nkeys, tl), lambda i: (0, 0, i))],
        out_specs=(spec, spec, spec, spec),
        compiler_params=pltpu.CompilerParams(dimension_semantics=("arbitrary",),
                                             vmem_limit_bytes=VMEM_LIMIT_BYTES),
        name="peer_gate",
    )(sc)


def _peer_kernel(x1t_ref, x1_ref, u_ref, vt_ref, w1_ref, nsel_ref, w2_ref, rank_ref, g_ref, b_ref, o_ref,
                 acc_ref, act_ref, ga_ref, w2s_ref, ranks_ref, *, alpha, nkeys):
    e = pl.program_id(1)
    heads = w2_ref.shape[0]
    n_i = w1_ref.shape[1]
    tm = x1t_ref.shape[1]
    n_lt = tm // LANES

    @pl.when(e == 0)
    def _():
        acc_ref[...] = jnp.zeros_like(acc_ref)
        for c in range(n_lt):
            w2s_ref[:, c] = w2_ref[:, :, c * LANES:(c + 1) * LANES]
            ranks_ref[:, c] = rank_ref[:, :, c * LANES:(c + 1) * LANES]

    act = jnp.dot(u_ref[...], x1t_ref[...], preferred_element_type=F32)
    for c in range(n_lt):
        act_ref[c] = act[:, c * LANES:(c + 1) * LANES]

    def chunk(ils, j0, c):
        lanes = pl.ds(pl.multiple_of(c * LANES, LANES), LANES)
        keys = pl.ds(j0, CHUNK_ROWS)
        gates = [jnp.zeros((CHUNK_ROWS, LANES), BF16) for _ in ils]
        for h in range(heads):
            rank = ranks_ref[h, c, keys, :]
            w2 = w2s_ref[h, c, keys, :]
            for n, il in enumerate(ils):
                nsel = nsel_ref[h, il:il + 1, lanes].astype(BF16)
                w1 = w1_ref[h, il:il + 1, lanes].astype(BF16)
                gates[n] = gates[n] + jnp.where(rank < nsel, w2, jnp.zeros_like(w2)) * w1
        for n, il in enumerate(ils):
            rows = pl.ds(il * nkeys + j0, CHUNK_ROWS)
            a = act_ref[c, rows, :]
            gelu2 = a + a * lax.erf(a * math.sqrt(0.5))
            ga_ref[rows, lanes] = gates[n] * gelu2.astype(BF16)

    for il in range(0, n_i, CHUNK_KEYS):
        for j0 in range(0, nkeys, CHUNK_ROWS):
            def lane_tile(c, carry, ils=tuple(range(il, il + CHUNK_KEYS)), j0=j0):
                chunk(ils, j0, c)
                return carry
            lax.fori_loop(0, n_lt, lane_tile, 0)

    acc_ref[...] += jnp.dot(vt_ref[0], ga_ref[...], preferred_element_type=F32)

    @pl.when(e == pl.num_programs(1) - 1)
    def _():
        z = alpha * x1_ref[...] + acc_ref[...].T
        o_ref[...] = _layer_norm(z, g_ref[...], b_ref[...])


def _peer_dense(x1t, x1, u_b, vt_blk, w1, nsel, w2, rank, ln_g, ln_b, *, alpha, tm=1024):
    t, d = x1.shape
    n_eblk, _, be = vt_blk.shape
    heads, nkeys, _ = w2.shape
    n_i = be // nkeys
    n_lt = tm // LANES
    kern = functools.partial(_peer_kernel, alpha=alpha, nkeys=nkeys)
    return pl.pallas_call(
        kern,
        out_shape=jax.ShapeDtypeStruct((t, d), F32),
        grid=(t // tm, n_eblk),
        in_specs=[pl.BlockSpec((d, tm), lambda i, e: (0, i)),
                  pl.BlockSpec((tm, d), lambda i, e: (i, 0)),
                  pl.BlockSpec((be, d), lambda i, e: (e, 0)),
                  pl.BlockSpec((1, d, be), lambda i, e: (e, 0, 0)),
                  pl.BlockSpec((heads, n_i, tm), lambda i, e: (0, e, i)),
                  pl.BlockSpec((heads, n_i, tm), lambda i, e: (0, e, i)),
                  pl.BlockSpec((heads, nkeys, tm), lambda i, e: (0, 0, i)),
                  pl.BlockSpec((heads, nkeys, tm), lambda i, e: (0, 0, i)),
                  pl.BlockSpec(ln_g.shape, lambda i, e: (0, 0)),
                  pl.BlockSpec(ln_b.shape, lambda i, e: (0, 0))],
        out_specs=pl.BlockSpec((tm, d), lambda i, e: (i, 0)),
        scratch_shapes=[pltpu.VMEM((d, tm), F32), pltpu.VMEM((n_lt, be, LANES), F32),
                        pltpu.VMEM((be, tm), BF16),
                        pltpu.VMEM((heads, n_lt, nkeys, LANES), BF16),
                        pltpu.VMEM((heads, n_lt, nkeys, LANES), BF16)],
        compiler_params=pltpu.CompilerParams(dimension_semantics=("arbitrary", "arbitrary"),
                                             vmem_limit_bytes=VMEM_LIMIT_BYTES),
        name="peer_dense",
    )(x1t, x1, u_b, vt_blk, w1, nsel, w2, rank, ln_g, ln_b)


def kernel(x, w_in, conv_w, conv_b, lam_q1, lam_k1, lam_q2, lam_k2, subln_g, w_out, ln1_g, ln1_b,
           peer_wq, peer_keys, peer_u, peer_v, ln2_g, ln2_b):
    batch, seq, d = x.shape
    depth = w_in.shape[0]
    d_conv = conv_w.shape[2]
    att_vdim = subln_g.shape[1]
    d_v = N_ATT_HEADS * att_vdim
    d_qk = (w_in.shape[2] - 3 * d_conv - d_v) // 2
    qk_dim = d_qk // (2 * N_ATT_HEADS)
    alpha = (2 * depth) ** 0.25
    heads, _, nkeys, dhalf = peer_keys.shape[1:]
    row = lambda a: a.reshape(1, -1)

    x2 = x.reshape(batch * seq, d)
    for l in range(depth):
        lam_init = 0.8 - 0.6 * math.exp(-0.3 * l)
        qscale = LOG2E / math.sqrt(qk_dim)
        yconv, q, k, v = _inproj_conv(x2, w_in[l].astype(BF16), conv_w[l], row(conv_b[l]), seq=seq,
                                      d_conv=d_conv, d_qk=d_qk, d_v=d_v, qscale=qscale)
        yatt = _diff_attn(q, k, v, row(lam_q1[l]), row(lam_k1[l]), row(lam_q2[l]), row(lam_k2[l]),
                          row(subln_g[l]), batch=batch, seq=seq, lam_init=lam_init)
        x1, x1t, sc = _outproj_query(x2, yconv, yatt, w_out[l].astype(BF16), row(ln1_g[l]), row(ln1_b[l]),
                                     peer_wq[l].astype(BF16),
                                     peer_keys[l].reshape(heads * 2, nkeys, dhalf).astype(BF16), alpha=alpha)
        w1, nsel, w2, rank = _peer_gate(sc, topk=PEER_TOPK)
        vt_blk = peer_v[l].reshape(-1, PEER_EXPERT_BLOCK, d).transpose(0, 2, 1).astype(BF16)
        x2 = _peer_dense(x1t, x1, peer_u[l].astype(BF16), vt_blk, w1, nsel, w2, rank,
                         row(ln2_g[l]), row(ln2_b[l]), alpha=alpha)
    return x2.reshape(batch, seq, d)
```

```python
import functools
import math

import jax
import jax.numpy as jnp
from jax import lax
from jax.experimental import pallas as pl
from jax.experimental.pallas import tpu as pltpu

F32 = jnp.float32
BF16 = jnp.bfloat16

CONV_WIDTH = 3
N_ATT_HEADS = 4
PEER_HEADS = 8
PEER_NKEYS = 128
PEER_TOPK = 16
PEER_EXPERT_BLOCK = 1024
CHUNK_KEYS = 8
CHUNK_ROWS = 128
LN_EPS = 1e-5
LOG2E = 1.4426950408889634
NEG_BIG = -1e30

SUBLANES = 8
LANES = 128
VMEM_LIMIT_BYTES = 60 * 1024 * 1024


def _nt_dot(a, b):
    return lax.dot_general(a, b, (((1,), (1,)), ((), ())), preferred_element_type=F32)


def _inproj_kernel(x_ref, w_ref, cw_ref, cb_ref, yconv_ref, q_ref, k_ref, v_ref, carry_ref,
                   *, tiles_per_seq, d_conv, d_qk, n_heads, qscale):
    i = pl.program_id(0)

    @pl.when(i % tiles_per_seq == 0)
    def _():
        carry_ref[...] = jnp.zeros_like(carry_ref)

    xb = x_ref[...].astype(BF16)

    def proj(lo, width):
        return jnp.dot(xb, w_ref[:, lo:lo + width], preferred_element_type=F32)

    gb = proj(0, d_conv)
    u = proj(d_conv, d_conv) * proj(2 * d_conv, d_conv)
    tm = u.shape[0]
    row = lax.broadcasted_iota(jnp.int32, u.shape, 0)
    c1 = carry_ref[SUBLANES - 1:SUBLANES, :]
    c2 = carry_ref[SUBLANES - 2:SUBLANES - 1, :]
    u1 = jnp.where(row == 0, c1, pltpu.roll(u, 1, 0))
    u2 = jnp.where(row == 0, c2, jnp.where(row == 1, c1, pltpu.roll(u, 2, 0)))
    conv = cb_ref[...] + cw_ref[2:3, :] * u + cw_ref[1:2, :] * u1 + cw_ref[0:1, :] * u2
    carry_ref[...] = u[tm - SUBLANES:, :]
    yconv_ref[...] = (gb * conv).astype(yconv_ref.dtype)

    base = 3 * d_conv
    q = (proj(base, d_qk) * qscale).astype(q_ref.dtype)
    hq = d_qk // n_heads
    lane = lax.broadcasted_iota(jnp.int32, (tm, hq), 1)
    zero = jnp.zeros((tm, hq), q_ref.dtype)
    for h in range(n_heads):
        qh = q[:, h * hq:(h + 1) * hq]
        q_ref[:, (2 * h) * hq:(2 * h + 1) * hq] = jnp.where(lane < hq // 2, qh, zero)
        q_ref[:, (2 * h + 1) * hq:(2 * h + 2) * hq] = jnp.where(lane >= hq // 2, qh, zero)
    k_ref[...] = proj(base + d_qk, d_qk).astype(k_ref.dtype)
    v_ref[...] = proj(base + 2 * d_qk, v_ref.shape[1]).astype(v_ref.dtype)


def _inproj_conv(x2, w_in_b, conv_w, conv_b, *, seq, d_conv, d_qk, d_v, qscale, tm=1024):
    t, d = x2.shape
    assert seq % tm == 0
    kern = functools.partial(_inproj_kernel, tiles_per_seq=seq // tm, d_conv=d_conv, d_qk=d_qk,
                             n_heads=N_ATT_HEADS, qscale=qscale)
    return pl.pallas_call(
        kern,
        out_shape=(jax.ShapeDtypeStruct((t, d_conv), BF16), jax.ShapeDtypeStruct((t, 2 * d_qk), BF16),
                   jax.ShapeDtypeStruct((t, d_qk), BF16), jax.ShapeDtypeStruct((t, d_v), BF16)),
        grid=(t // tm,),
        in_specs=[pl.BlockSpec((tm, d), lambda i: (i, 0)),
                  pl.BlockSpec(w_in_b.shape, lambda i: (0, 0)),
                  pl.BlockSpec(conv_w.shape, lambda i: (0, 0)),
                  pl.BlockSpec(conv_b.shape, lambda i: (0, 0))],
        out_specs=(pl.BlockSpec((tm, d_conv), lambda i: (i, 0)), pl.BlockSpec((tm, 2 * d_qk), lambda i: (i, 0)),
                   pl.BlockSpec((tm, d_qk), lambda i: (i, 0)), pl.BlockSpec((tm, d_v), lambda i: (i, 0))),
        scratch_shapes=[pltpu.VMEM((SUBLANES, d_conv), F32)],
        compiler_params=pltpu.CompilerParams(dimension_semantics=("arbitrary",),
                                             vmem_limit_bytes=VMEM_LIMIT_BYTES),
        name="inproj_conv",
    )(x2, w_in_b, conv_w, conv_b)


def _attn_kernel(lq1_ref, lk1_ref, lq2_ref, lk2_ref, g_ref, q_ref, k_ref, v_ref, o_ref,
                 m_ref, l_ref, acc_ref, *, tq, n_heads, lam_init):
    qi = pl.program_id(1)
    hw = k_ref.shape[1] // n_heads
    n_chains = 2 * n_heads

    m_ref[...] = jnp.full(m_ref.shape, NEG_BIG, F32)
    l_ref[...] = jnp.zeros(l_ref.shape, F32)
    acc_ref[...] = jnp.zeros(acc_ref.shape, F32)

    def block(j, diagonal):
        off = pl.multiple_of(j * tq, tq)
        n_slab = tq // LANES
        scores = [_nt_dot(q_ref[:, c * hw:(c + 1) * hw], k_ref[pl.ds(off, tq), (c // 2) * hw:(c // 2 + 1) * hw])
                  for c in range(n_chains)]
        probs, alphas = [], []
        for c in range(n_chains):
            slabs = [scores[c][:, t * LANES:(t + 1) * LANES] for t in range(n_slab)]
            if diagonal:
                r = lax.broadcasted_iota(jnp.int32, (tq, LANES), 0)
                col = lax.broadcasted_iota(jnp.int32, (tq, LANES), 1)
                slabs = [jnp.where(col + t * LANES <= r, sl, NEG_BIG) for t, sl in enumerate(slabs)]
            mx = slabs[0]
            for sl in slabs[1:]:
                mx = jnp.maximum(mx, sl)
            m_old = m_ref[c]
            m_new = jnp.maximum(m_old, jnp.broadcast_to(jnp.max(mx, axis=1, keepdims=True), m_old.shape))
            alpha = jnp.exp2(m_old - m_new)
            ps = [jnp.exp2(sl - m_new) for sl in slabs]
            part = ps[0]
            for p in ps[1:]:
                part = part + p
            m_ref[c] = m_new
            l_ref[c] = alpha * l_ref[c] + part
            probs.append(jnp.concatenate(ps, axis=1).astype(v_ref.dtype))
            alphas.append(alpha)
        for c in range(n_chains):
            vc = v_ref[pl.ds(off, tq), (c // 2) * hw:(c // 2 + 1) * hw]
            acc_ref[c] = alphas[c] * acc_ref[c] + jnp.dot(probs[c], vc, preferred_element_type=F32)

    def body(j, carry):
        block(j, False)
        return carry

    lax.fori_loop(0, qi, body, 0)
    block(qi, True)

    lam = (jnp.exp(jnp.sum(lq1_ref[...] * lk1_ref[...], axis=1, keepdims=True))
           - jnp.exp(jnp.sum(lq2_ref[...] * lk2_ref[...], axis=1, keepdims=True)) + lam_init)
    for h in range(n_heads):
        o1 = acc_ref[2 * h] / jnp.sum(l_ref[2 * h], axis=1, keepdims=True)
        o2 = acc_ref[2 * h + 1] / jnp.sum(l_ref[2 * h + 1], axis=1, keepdims=True)
        y = o1 - lam * o2
        y = y * lax.rsqrt(jnp.mean(y * y, axis=1, keepdims=True) + LN_EPS) * g_ref[...]
        o_ref[:, h * hw:(h + 1) * hw] = (y * (1.0 - lam_init)).astype(o_ref.dtype)


def _diff_attn(q, k, v, lam_q1, lam_k1, lam_q2, lam_k2, subln_g, *, batch, seq, lam_init, tq=512):
    t, d_k = k.shape
    d_v = v.shape[1]
    n_heads = N_ATT_HEADS
    assert d_k == d_v and q.shape[1] == 2 * d_k
    nq = seq // tq
    kern = functools.partial(_attn_kernel, tq=tq, n_heads=n_heads, lam_init=lam_init)
    small = lambda a: pl.BlockSpec(a.shape, lambda b, i: (0, 0))
    n_chains = 2 * n_heads
    return pl.pallas_call(
        kern,
        out_shape=jax.ShapeDtypeStruct((t, d_v), BF16),
        grid=(batch, nq),
        in_specs=[small(lam_q1), small(lam_k1), small(lam_q2), small(lam_k2), small(subln_g),
                  pl.BlockSpec((tq, 2 * d_k), lambda b, i: (b * nq + i, 0)),
                  pl.BlockSpec((seq, d_k), lambda b, i: (b, 0)),
                  pl.BlockSpec((seq, d_v), lambda b, i: (b, 0))],
        out_specs=pl.BlockSpec((tq, d_v), lambda b, i: (b * nq + i, 0)),
        scratch_shapes=[pltpu.VMEM((n_chains, tq, LANES), F32), pltpu.VMEM((n_chains, tq, LANES), F32),
                        pltpu.VMEM((n_chains, tq, d_v // n_heads), F32)],
        compiler_params=pltpu.CompilerParams(dimension_semantics=("arbitrary", "arbitrary"),
                                             vmem_limit_bytes=VMEM_LIMIT_BYTES),
        name="diff_attn",
    )(lam_q1, lam_k1, lam_q2, lam_k2, subln_g, q, k, v)


def _layer_norm(z, g, b):
    mu = jnp.mean(z, axis=1, keepdims=True)
    zc = z - mu
    var = jnp.mean(zc * zc, axis=1, keepdims=True)
    return zc * lax.rsqrt(var + LN_EPS) * g + b


def _outproj_kernel(x_ref, yc_ref, ya_ref, wo_ref, g_ref, b_ref, wq_ref, keys_ref,
                    x1_ref, x1t_ref, sc_ref, *, alpha, d_conv):
    ymix = (jnp.dot(yc_ref[...], wo_ref[:d_conv, :], preferred_element_type=F32)
            + jnp.dot(ya_ref[...], wo_ref[d_conv:, :], preferred_element_type=F32))
    x1 = _layer_norm(alpha * x_ref[...] + ymix, g_ref[...], b_ref[...])
    x1_ref[...] = x1
    x1t_ref[...] = x1.T.astype(BF16)
    n_sub, _, dh = keys_ref.shape
    q = jnp.dot(x1.astype(BF16), wq_ref[...], preferred_element_type=F32).astype(BF16)
    for s in range(n_sub):
        sc_ref[s] = _nt_dot(keys_ref[s], q[:, s * dh:(s + 1) * dh])


def _outproj_query(x2, yconv, yatt, w_out_b, ln_g, ln_b, wq_b, keys_b, *, alpha, tm=1024):
    t, d = x2.shape
    d_conv = yconv.shape[1]
    n_sub, nkeys, _ = keys_b.shape
    kern = functools.partial(_outproj_kernel, alpha=alpha, d_conv=d_conv)
    full = lambda a: pl.BlockSpec(a.shape, lambda i: (0,) * a.ndim)
    return pl.pallas_call(
        kern,
        out_shape=(jax.ShapeDtypeStruct((t, d), F32), jax.ShapeDtypeStruct((d, t), BF16),
                   jax.ShapeDtypeStruct((n_sub, nkeys, t), F32)),
        grid=(t // tm,),
        in_specs=[pl.BlockSpec((tm, d), lambda i: (i, 0)),
                  pl.BlockSpec((tm, d_conv), lambda i: (i, 0)),
                  pl.BlockSpec((tm, yatt.shape[1]), lambda i: (i, 0)),
                  full(w_out_b), full(ln_g), full(ln_b), full(wq_b), full(keys_b)],
        out_specs=(pl.BlockSpec((tm, d), lambda i: (i, 0)), pl.BlockSpec((d, tm), lambda i: (0, i)),
                   pl.BlockSpec((n_sub, nkeys, tm), lambda i: (0, 0, i))),
        compiler_params=pltpu.CompilerParams(dimension_semantics=("arbitrary",),
                                             vmem_limit_bytes=VMEM_LIMIT_BYTES),
        name="outproj_query",
    )(x2, yconv, yatt, w_out_b, ln_g, ln_b, wq_b, keys_b)


def _oddeven_merge_sort_pairs(n):
    pairs = []
    p = 1
    while p < n:
        k = p
        while k >= 1:
            for j in range(k % p, n - k, 2 * k):
                for i in range(min(k, n - j - k)):
                    if (i + j) // (2 * p) == (i + j + k) // (2 * p):
                        pairs.append((i + j, i + j + k))
            k //= 2
        p *= 2
    return pairs


def _sort_desc(vals):
    vals = list(vals)
    for i, j in _oddeven_merge_sort_pairs(len(vals)):
        hi = jnp.maximum(vals[i], vals[j])
        lo = jnp.minimum(vals[i], vals[j])
        vals[i], vals[j] = hi, lo
    return vals


def _bitonic_merge_desc(vals):
    vals = list(vals)
    n = len(vals)
    d = n // 2
    while d >= 1:
        for i in range(n):
            if i & d == 0:
                hi = jnp.maximum(vals[i], vals[i + d])
                lo = jnp.minimum(vals[i], vals[i + d])
                vals[i], vals[i + d] = hi, lo
        d //= 2
    return vals


def _top_sorted(s, k):
    n = s.shape[0]
    assert n == k * SUBLANES
    slabs = _sort_desc([s[a * SUBLANES:(a + 1) * SUBLANES, :] for a in range(k)])
    shift = SUBLANES // 2
    while shift >= 1:
        other = [pltpu.roll(x, shift, 0) for x in slabs]
        slabs = _bitonic_merge_desc([jnp.maximum(slabs[r], other[k - 1 - r]) for r in range(k)])
        shift //= 2
    return [x[0:1, :] for x in slabs]


def _next_below(s, kth, k):
    ge = s >= kth
    cnt = jnp.sum(jnp.where(ge, 1.0, 0.0), axis=0, keepdims=True)
    below = jnp.max(jnp.where(ge, NEG_BIG, s), axis=0, keepdims=True)
    return jnp.where(cnt > k, kth, below)


def _count_above(tops, x, *, strict):
    k = len(tops)
    levels = k.bit_length() - 1
    assert k == 1 << levels
    below = (lambda a, p: a < p) if strict else (lambda a, p: a <= p)
    conds = []

    def pivot(level, i, base):
        if i == level:
            return tops[base + (k >> (level + 1)) - 1]
        return jnp.where(conds[i], pivot(level, i + 1, base + (k >> (i + 1))), pivot(level, i + 1, base))

    for level in range(levels):
        conds.append(below(x, pivot(level, 0, 0)))
    count = jnp.zeros_like(x)
    for i, c in enumerate(conds):
        count = count + jnp.where(c, float(k >> (i + 1)), 0.0)
    return jnp.where(below(x, tops[k - 1]), float(k), count)


def _gate_kernel(sc_ref, w1_ref, nsel_ref, w2_ref, rank_ref, *, topk):
    n_sub = sc_ref.shape[0]
    heads = n_sub // 2
    k = topk
    tops = []
    for s_idx in range(n_sub):
        s = sc_ref[s_idx]
        rows = _top_sorted(s, k)
        rows.append(_next_below(s, rows[k - 1], k))
        tops.append([jnp.exp(r - rows[0]) for r in rows])
    a = [jnp.concatenate([tops[2 * h][r] for h in range(heads)], axis=0) for r in range(k + 1)]
    b = [jnp.concatenate([tops[2 * h + 1][r] for h in range(heads)], axis=0) for r in range(k + 1)]
    cands = [a[r] * b[c] for r in range(k + 1) for c in range(k + 1) if (r + 1) * (c + 1) <= k + 1]
    n_pad = 1
    while n_pad < len(cands):
        n_pad *= 2
    pad = jnp.full_like(cands[0], -1.0)
    srt = _sort_desc(cands + [pad] * (n_pad - len(cands)))
    tstar = 0.5 * (srt[k - 1] + srt[k])
    z = jnp.zeros_like(tstar)
    for c in cands:
        z = z + jnp.where(c >= tstar, c, 0.0)
    zinv = 1.0 / z
    for h in range(heads):
        e1 = jnp.exp(sc_ref[2 * h] - jnp.max(sc_ref[2 * h], axis=0, keepdims=True))
        e2 = jnp.exp(sc_ref[2 * h + 1] - jnp.max(sc_ref[2 * h + 1], axis=0, keepdims=True))
        theta = tstar[h:h + 1, :] / e1
        top2 = tops[2 * h + 1][:k]
        rank = _count_above(top2, e2, strict=True)
        nsel = _count_above(top2, theta, strict=False)
        w1_ref[h] = e1
        nsel_ref[h] = nsel
        w2_ref[h] = (e2 * (0.5 * zinv[h:h + 1, :])).astype(w2_ref.dtype)
        rank_ref[h] = rank.astype(rank_ref.dtype)


def _peer_gate(sc, *, topk, tl=256):
    n_sub, nkeys, t = sc.shape
    heads = n_sub // 2
    out_f32 = jax.ShapeDtypeStruct((heads, nkeys, t), F32)
    out_b16 = jax.ShapeDtypeStruct((heads, nkeys, t), BF16)
    spec = pl.BlockSpec((heads, nkeys, tl), lambda i: (0, 0, i))
    return pl.pallas_call(
        functools.partial(_gate_kernel, topk=topk),
        out_shape=(out_f32, out_f32, out_b16, out_b16),
        grid=(t // tl,),
        in_specs=[pl.BlockSpec((n_sub, nkeys, tl), lambda i: (0, 0, i))],
        out_specs=(spec, spec, spec, spec),
        compiler_params=pltpu.CompilerParams(dimension_semantics=("arbitrary",),
                                             vmem_limit_bytes=VMEM_LIMIT_BYTES),
        name="peer_gate",
    )(sc)


def _peer_kernel(x1t_ref, x1_ref, u_ref, vt_ref, w1_ref, nsel_ref, w2_ref, rank_ref, g_ref, b_ref, o_ref,
                 acc_ref, act_ref, ga_ref, w2s_ref, ranks_ref, *, alpha, nkeys):
    e = pl.program_id(1)
    heads = w2_ref.shape[0]
    n_i = w1_ref.shape[1]
    tm = x1t_ref.shape[1]
    n_lt = tm // LANES

    @pl.when(e == 0)
    def _():
        acc_ref[...] = jnp.zeros_like(acc_ref)
        for c in range(n_lt):
            w2s_ref[:, c] = w2_ref[:, :, c * LANES:(c + 1) * LANES]
            ranks_ref[:, c] = rank_ref[:, :, c * LANES:(c + 1) * LANES]

    act = jnp.dot(u_ref[...], x1t_ref[...], preferred_element_type=F32)
    for c in range(n_lt):
        act_ref[c] = act[:, c * LANES:(c + 1) * LANES]

    def chunk(ils, j0, c):
        lanes = pl.ds(pl.multiple_of(c * LANES, LANES), LANES)
        keys = pl.ds(j0, CHUNK_ROWS)
        gates = [jnp.zeros((CHUNK_ROWS, LANES), BF16) for _ in ils]
        for h in range(heads):
            rank = ranks_ref[h, c, keys, :]
            w2 = w2s_ref[h, c, keys, :]
            for n, il in enumerate(ils):
                nsel = nsel_ref[h, il:il + 1, lanes].astype(BF16)
                w1 = w1_ref[h, il:il + 1, lanes].astype(BF16)
                gates[n] = gates[n] + jnp.where(rank < nsel, w2, jnp.zeros_like(w2)) * w1
        for n, il in enumerate(ils):
            rows = pl.ds(il * nkeys + j0, CHUNK_ROWS)
            a = act_ref[c, rows, :]
            gelu2 = a + a * lax.erf(a * math.sqrt(0.5))
            ga_ref[rows, lanes] = gates[n] * gelu2.astype(BF16)

    for il in range(0, n_i, CHUNK_KEYS):
        for j0 in range(0, nkeys, CHUNK_ROWS):
            def lane_tile(c, carry, ils=tuple(range(il, il + CHUNK_KEYS)), j0=j0):
                chunk(ils, j0, c)
                return carry
            lax.fori_loop(0, n_lt, lane_tile, 0)

    acc_ref[...] += jnp.dot(vt_ref[0], ga_ref[...], preferred_element_type=F32)

    @pl.when(e == pl.num_programs(1) - 1)
    def _():
        z = alpha * x1_ref[...] + acc_ref[...].T
        o_ref[...] = _layer_norm(z, g_ref[...], b_ref[...])


def _peer_dense(x1t, x1, u_b, vt_blk, w1, nsel, w2, rank, ln_g, ln_b, *, alpha, tm=1024):
    t, d = x1.shape
    n_eblk, _, be = vt_blk.shape
    heads, nkeys, _ = w2.shape
    n_i = be // nkeys
    n_lt = tm // LANES
    kern = functools.partial(_peer_kernel, alpha=alpha, nkeys=nkeys)
    return pl.pallas_call(
        kern,
        out_shape=jax.ShapeDtypeStruct((t, d), F32),
        grid=(t // tm, n_eblk),
        in_specs=[pl.BlockSpec((d, tm), lambda i, e: (0, i)),
                  pl.BlockSpec((tm, d), lambda i, e: (i, 0)),
                  pl.BlockSpec((be, d), lambda i, e: (e, 0)),
                  pl.BlockSpec((1, d, be), lambda i, e: (e, 0, 0)),
                  pl.BlockSpec((heads, n_i, tm), lambda i, e: (0, e, i)),
                  pl.BlockSpec((heads, n_i, tm), lambda i, e: (0, e, i)),
                  pl.BlockSpec((heads, nkeys, tm), lambda i, e: (0, 0, i)),
                  pl.BlockSpec((heads, nkeys, tm), lambda i, e: (0, 0, i)),
                  pl.BlockSpec(ln_g.shape, lambda i, e: (0, 0)),
                  pl.BlockSpec(ln_b.shape, lambda i, e: (0, 0))],
        out_specs=pl.BlockSpec((tm, d), lambda i, e: (i, 0)),
        scratch_shapes=[pltpu.VMEM((d, tm), F32), pltpu.VMEM((n_lt, be, LANES), F32),
                        pltpu.VMEM((be, tm), BF16),
                        pltpu.VMEM((heads, n_lt, nkeys, LANES), BF16),
                        pltpu.VMEM((heads, n_lt, nkeys, LANES), BF16)],
        compiler_params=pltpu.CompilerParams(dimension_semantics=("arbitrary", "arbitrary"),
                                             vmem_limit_bytes=VMEM_LIMIT_BYTES),
        name="peer_dense",
    )(x1t, x1, u_b, vt_blk, w1, nsel, w2, rank, ln_g, ln_b)


def kernel(x, w_in, conv_w, conv_b, lam_q1, lam_k1, lam_q2, lam_k2, subln_g, w_out, ln1_g, ln1_b,
           peer_wq, peer_keys, peer_u, peer_v, ln2_g, ln2_b):
    batch, seq, d = x.shape
    depth = w_in.shape[0]
    d_conv = conv_w.shape[2]
    att_vdim = subln_g.shape[1]
    d_v = N_ATT_HEADS * att_vdim
    d_qk = (w_in.shape[2] - 3 * d_conv - d_v) // 2
    qk_dim = d_qk // (2 * N_ATT_HEADS)
    alpha = (2 * depth) ** 0.25
    heads, _, nkeys, dhalf = peer_keys.shape[1:]
    row = lambda a: a.reshape(1, -1)

    x2 = x.reshape(batch * seq, d)
    for l in range(depth):
        lam_init = 0.8 - 0.6 * math.exp(-0.3 * l)
        qscale = LOG2E / math.sqrt(qk_dim)
        yconv, q, k, v = _inproj_conv(x2, w_in[l].astype(BF16), conv_w[l], row(conv_b[l]), seq=seq,
                                      d_conv=d_conv, d_qk=d_qk, d_v=d_v, qscale=qscale)
        yatt = _diff_attn(q, k, v, row(lam_q1[l]), row(lam_k1[l]), row(lam_q2[l]), row(lam_k2[l]),
                          row(subln_g[l]), batch=batch, seq=seq, lam_init=lam_init)
        x1, x1t, sc = _outproj_query(x2, yconv, yatt, w_out[l].astype(BF16), row(ln1_g[l]), row(ln1_b[l]),
                                     peer_wq[l].astype(BF16),
                                     peer_keys[l].reshape(heads * 2, nkeys, dhalf).astype(BF16), alpha=alpha)
        w1, nsel, w2, rank = _peer_gate(sc, topk=PEER_TOPK)
        vt_blk = peer_v[l].reshape(-1, PEER_EXPERT_BLOCK, d).transpose(0, 2, 1).astype(BF16)
        x2 = _peer_dense(x1t, x1, peer_u[l].astype(BF16), vt_blk, w1, nsel, w2, rank,
                         row(ln2_g[l]), row(ln2_b[l]), alpha=alpha)
    return x2.reshape(batch, seq, d)
```
